```python
import jax, jax.numpy as jnp
from jax import lax
import numpy as np

D_MODEL = 2048
BATCH = 4
SEQ = 4096
DEPTH = 2

GRID_W = 64
CTX_LEN = 256
N_MIXERS = 2
N_MLA_LAYERS = (DEPTH + 1) // 2
N_NA_LAYERS = DEPTH // 2
MLA_HEADS = 16
MLA_Q_RANK = 512
MLA_KV_RANK = 512
MLA_NOPE = 128
MLA_ROPE = 64
MLA_V = 128
NA_HEADS = 16
NA_HEAD_DIM = D_MODEL // NA_HEADS
NA_WIN_R = 8
NA_WIN_C = 16
D_FF = -(-(8 * D_MODEL) // (3 * 256)) * 256
ROPE_THETA = 10000.0
EPS = 1e-6
Q_BLOCK = 128

kernel_name = "hybrid_mla_natten_dit_block"


def rmsnorm(x, g):
    x32 = x.astype(jnp.float32)
    y = x32 * lax.rsqrt(jnp.mean(x32 * x32, axis=-1, keepdims=True) + EPS)
    return (y * g.astype(jnp.float32)).astype(x.dtype)


def modulate(h, shift, scale):
    return h * (1 + scale) + shift


def swiglu(h, w_gate, w_up, w_down):
    return (jax.nn.silu(h @ w_gate) * (h @ w_up)) @ w_down


def attention(q, k, v, scale):
    s = jnp.einsum('bqhd,bkhd->bhqk', q, k) * scale
    p = jax.nn.softmax(s.astype(jnp.float32), axis=-1).astype(v.dtype)
    return jnp.einsum('bhqk,bkhd->bqhd', p, v)


def blocked_attention(q, k, v, scale):
    B, S, H, Dq = q.shape
    nb = S // Q_BLOCK
    qb = q.reshape(B, nb, Q_BLOCK, H, Dq).transpose(1, 0, 2, 3, 4)
    ob = lax.map(lambda qi: attention(qi, k, v, scale), qb)
    return ob.transpose(1, 0, 2, 3, 4).reshape(B, S, H, v.shape[-1])


def axial_rope_tables(S, dtype):
    t = jnp.arange(S)
    row = (t // GRID_W).astype(jnp.float32)
    col = (t % GRID_W).astype(jnp.float32)
    n_freq = MLA_ROPE // 4
    inv_freq = ROPE_THETA ** (-jnp.arange(n_freq, dtype=jnp.float32) / n_freq)
    ang_r = row[:, None] * inv_freq
    ang_c = col[:, None] * inv_freq
    return (jnp.cos(ang_r)[:, None, :].astype(dtype), jnp.sin(ang_r)[:, None, :].astype(dtype),
            jnp.cos(ang_c)[:, None, :].astype(dtype), jnp.sin(ang_c)[:, None, :].astype(dtype))


def rope_1d(x, cos, sin):
    x1, x2 = jnp.split(x, 2, axis=-1)
    return jnp.concatenate([x1 * cos - x2 * sin, x2 * cos + x1 * sin], axis=-1)


def rope_2d(x, rope):
    cos_r, sin_r, cos_c, sin_c = rope
    xr, xc = jnp.split(x, 2, axis=-1)
    return jnp.concatenate([rope_1d(xr, cos_r, sin_r), rope_1d(xc, cos_c, sin_c)], axis=-1)


def _mla_q(t, w_dq, q_norm, w_uq):
    B, L, _ = t.shape
    cq = rmsnorm(t @ w_dq, q_norm)
    q = (cq @ w_uq).reshape(B, L, MLA_HEADS, MLA_NOPE + MLA_ROPE)
    return q[..., :MLA_NOPE], q[..., MLA_NOPE:]


def _mla_kv(t, w_dkv, kv_norm, w_ukv):
    B, L, _ = t.shape
    kv_a = t @ w_dkv
    c_kv = rmsnorm(kv_a[..., :MLA_KV_RANK], kv_norm)
    k_pe = kv_a[..., MLA_KV_RANK:][:, :, None, :]
    kv = (c_kv @ w_ukv).reshape(B, L, MLA_HEADS, MLA_NOPE + MLA_V)
    return kv[..., :MLA_NOPE], k_pe, kv[..., MLA_NOPE:]


def _assemble(nope, pe):
    pe = jnp.broadcast_to(pe, nope.shape[:-1] + (pe.shape[-1],))
    return jnp.concatenate([nope, pe], axis=-1)


def mla_mixer(h, hc, w_dq, q_norm, w_uq, w_dkv, kv_norm, w_ukv, w_o, rope, with_ctx_out):
    B, S, _ = h.shape
    Lc = hc.shape[1]
    scale = (MLA_NOPE + MLA_ROPE) ** -0.5
    q_nope, q_pe = _mla_q(h, w_dq, q_norm, w_uq)
    k_nope, k_pe, v = _mla_kv(h, w_dkv, kv_norm, w_ukv)
    q = _assemble(q_nope, rope_2d(q_pe, rope))
    k = _assemble(k_nope, rope_2d(k_pe, rope))
    kc_nope, kc_pe, vc = _mla_kv(hc, w_dkv, kv_norm, w_ukv)
    kc = _assemble(kc_nope, kc_pe)
    keys = jnp.concatenate([kc, k], axis=1)
    vals = jnp.concatenate([vc, v], axis=1)
    y = blocked_attention(q, keys, vals, scale).reshape(B, S, MLA_HEADS * MLA_V) @ w_o
    yc = None
    if with_ctx_out:
        qc_nope, qc_pe = _mla_q(hc, w_dq, q_norm, w_uq)
        qc = _assemble(qc_nope, qc_pe)
        yc = attention(qc, kc, vc, scale).reshape(B, Lc, MLA_HEADS * MLA_V) @ w_o
    return y, yc


def na_mixer(h, hc, w_qkv, rel_bias, w_o, with_ctx_out):
    B, S, _ = h.shape
    Lc = hc.shape[1]
    rows = S // GRID_W
    kr = min(NA_WIN_R, rows)
    HD = NA_HEADS * NA_HEAD_DIM
    scale = NA_HEAD_DIM ** -0.5
    qkv = (h @ w_qkv).reshape(B, rows, GRID_W, 3, NA_HEADS, NA_HEAD_DIM)
    q_g, k_g, v_g = qkv[..., 0, :, :], qkv[..., 1, :, :], qkv[..., 2, :, :]
    kvc = (hc @ w_qkv[:, HD:]).reshape(B, Lc, 2, NA_HEADS, NA_HEAD_DIM)
    kc, vc = kvc[:, :, 0], kvc[:, :, 1]
    col = jnp.arange(GRID_W)
    col_start = jnp.clip(col - NA_WIN_C // 2, 0, GRID_W - NA_WIN_C)
    col_idx = col_start[:, None] + jnp.arange(NA_WIN_C)[None, :]
    dc = col_idx - col[:, None] + (NA_WIN_C - 1)
    bias_c = rel_bias[:, :, dc]
    n_loc = kr * NA_WIN_C

    def row_block(r):
        rs = jnp.clip(r - kr // 2, 0, rows - kr)
        q_r = lax.dynamic_index_in_dim(q_g, r, axis=1, keepdims=False)
        k_rows = lax.dynamic_slice_in_dim(k_g, rs, kr, axis=1)
        v_rows = lax.dynamic_slice_in_dim(v_g, rs, kr, axis=1)
        k_win = k_rows[:, :, col_idx]
        v_win = v_rows[:, :, col_idx]
        dr = rs + jnp.arange(kr) - r + (NA_WIN_R - 1)
        bias = bias_c[:, dr].transpose(0, 2, 1, 3)
        s_loc = jnp.einsum('bqhd,biqjhd->bhqij', q_r, k_win) * scale + bias[None]
        s_loc = s_loc.reshape(B, NA_HEADS, GRID_W, n_loc)
        s_ctx = jnp.einsum('bqhd,bkhd->bhqk', q_r, kc) * scale
        p = jax.nn.softmax(jnp.concatenate([s_loc, s_ctx], axis=-1).astype(jnp.float32), axis=-1).astype(v_g.dtype)
        p_loc = p[..., :n_loc].reshape(B, NA_HEADS, GRID_W, kr, NA_WIN_C)
        return (jnp.einsum('bhqij,biqjhd->bqhd', p_loc, v_win)
                + jnp.einsum('bhqk,bkhd->bqhd', p[..., n_loc:], vc))

    o = lax.map(row_block, jnp.arange(rows))
    y = o.transpose(1, 0, 2, 3, 4).reshape(B, S, HD) @ w_o
    yc = None
    if with_ctx_out:
        qc = (hc @ w_qkv[:, :HD]).reshape(B, Lc, NA_HEADS, NA_HEAD_DIM)
        yc = attention(qc, kc, vc, scale).reshape(B, Lc, HD) @ w_o
    return y, yc


def setup_inputs(seed: int = 0) -> dict:
    key = jax.random.key(seed)
    ks = jax.random.split(key, 24)
    f32 = jnp.float32
    D = D_MODEL

    def w(k, shape, fan_in, g=1.0):
        return jax.random.normal(k, shape, f32) * (g * fan_in ** -0.5)

    def gain(k, shape):
        return 1.0 + 0.05 * jax.random.normal(k, shape, f32)

    return {
        "x": jax.random.normal(ks[0], (BATCH, SEQ, D), f32),
        "c": jax.random.normal(ks[1], (BATCH, D), f32),
        "ctx": jax.random.normal(ks[2], (BATCH, CTX_LEN, D), f32),
        "c_ctx": jax.random.normal(ks[3], (D,), f32),
        "ada_w": w(ks[4], (DEPTH, D, 6 * D), D, 0.3),
        "ada_b": 0.02 * jax.random.normal(ks[5], (DEPTH, 6 * D), f32),
        "norm_mix": gain(ks[6], (DEPTH, D)),
        "norm_ffn": gain(ks[7], (DEPTH, D)),
        "norm_final": gain(ks[8], (D,)),
        "mla_w_dq": w(ks[9], (N_MLA_LAYERS, D, MLA_Q_RANK), D),
        "mla_q_norm": gain(ks[10], (N_MLA_LAYERS, MLA_Q_RANK)),
        "mla_w_uq": w(ks[11], (N_MLA_LAYERS, MLA_Q_RANK, MLA_HEADS * (MLA_NOPE + MLA_ROPE)), MLA_Q_RANK),
        "mla_w_dkv": w(ks[12], (N_MLA_LAYERS, D, MLA_KV_RANK + MLA_ROPE), D),
        "mla_kv_norm": gain(ks[13], (N_MLA_LAYERS, MLA_KV_RANK)),
        "mla_w_ukv": w(ks[14], (N_MLA_LAYERS, MLA_KV_RANK, MLA_HEADS * (MLA_NOPE + MLA_V)), MLA_KV_RANK),
        "mla_w_o": w(ks[15], (N_MLA_LAYERS, MLA_HEADS * MLA_V, D), MLA_HEADS * MLA_V),
        "na_w_qkv": w(ks[16], (N_NA_LAYERS, D, 3 * NA_HEADS * NA_HEAD_DIM), D),
        "na_rel_bias": 0.1 * jax.random.normal(ks[17], (N_NA_LAYERS, NA_HEADS, 2 * NA_WIN_R - 1, 2 * NA_WIN_C - 1), f32),
        "na_w_o": w(ks[18], (N_NA_LAYERS, NA_HEADS * NA_HEAD_DIM, D), NA_HEADS * NA_HEAD_DIM),
        "ffn_w_gate": w(ks[19], (DEPTH, D, D_FF), D),
        "ffn_w_up": w(ks[20], (DEPTH, D, D_FF), D),
        "ffn_w_down": w(ks[21], (DEPTH, D_FF, D), D_FF),
    }


def reference(x, c, ctx, c_ctx, ada_w, ada_b, norm_mix, norm_ffn, norm_final,
              mla_w_dq, mla_q_norm, mla_w_uq, mla_w_dkv, mla_kv_norm, mla_w_ukv, mla_w_o,
              na_w_qkv, na_rel_bias, na_w_o,
              ffn_w_gate, ffn_w_up, ffn_w_down):
    S = x.shape[1]
    rope = axial_rope_tables(S, x.dtype)
    xc = ctx
    for i in range(DEPTH):
        last = i == DEPTH - 1
        j = i // N_MIXERS
        m = jax.nn.silu(c) @ ada_w[i] + ada_b[i]
        mc = jax.nn.silu(c_ctx) @ ada_w[i] + ada_b[i]
        sh1, sc1, g1, sh2, sc2, g2 = jnp.split(m[:, None, :], 6, axis=-1)
        csh1, csc1, cg1, csh2, csc2, cg2 = jnp.split(mc, 6, axis=-1)
        h = modulate(rmsnorm(x, norm_mix[i]), sh1, sc1)
        hc = modulate(rmsnorm(xc, norm_mix[i]), csh1, csc1)
        if i % N_MIXERS == 0:
            y, yc = mla_mixer(h, hc, mla_w_dq[j], mla_q_norm[j], mla_w_uq[j], mla_w_dkv[j],
                              mla_kv_norm[j], mla_w_ukv[j], mla_w_o[j], rope, not last)
        else:
            y, yc = na_mixer(h, hc, na_w_qkv[j], na_rel_bias[j], na_w_o[j], not last)
        x = x + g1 * y
        x = x + g2 * swiglu(modulate(rmsnorm(x, norm_ffn[i]), sh2, sc2),
                            ffn_w_gate[i], ffn_w_up[i], ffn_w_down[i])
        if not last:
            xc = xc + cg1 * yc
            xc = xc + cg2 * swiglu(modulate(rmsnorm(xc, norm_ffn[i]), csh2, csc2),
                                   ffn_w_gate[i], ffn_w_up[i], ffn_w_down[i])
    return rmsnorm(x, norm_final)
```

```python
import functools

import jax
import jax.numpy as jnp
from jax import lax
from jax.experimental import pallas as pl
from jax.experimental.pallas import tpu as pltpu

GRID_W = 64
N_MIXERS = 2
MLA_NOPE = 128
MLA_ROPE = 64
MLA_V = 128
ROPE_THETA = 10000.0
EPS = 1e-6
NA_ROW_GROUP = 8
MASK_VALUE = -1e30

LANES = 128
V7X_VMEM_BYTES = 64 * 1024 * 1024
VMEM_LIMIT = V7X_VMEM_BYTES * 3 // 4

F32 = jnp.float32
BF16 = jnp.bfloat16
NT_DIMS = (((1,), (1,)), ((), ()))
TN_DIMS = (((0,), (0,)), ((), ()))


def _pick(n, pref):
    t = min(n, pref)
    while n % t:
        t //= 2
    return t


def _params(*sem):
    return pltpu.CompilerParams(dimension_semantics=sem, vmem_limit_bytes=VMEM_LIMIT)


def _rms(x):
    return x * lax.rsqrt(jnp.mean(x * x, axis=-1, keepdims=True) + EPS)


def _norm_mod(x, g, sh, sc):
    return (_rms(x) * g) * (1.0 + sc) + sh


def _silu(x):
    return x * (1.0 / (1.0 + jnp.exp(-x)))


def _mod_spec(k, d, tiles_per_row, const_row):
    if const_row is None:
        return pl.BlockSpec((None, None, 1, d), lambda i, *_: (i // tiles_per_row, k, 0, 0))
    return pl.BlockSpec((None, None, 1, d), lambda i, *_: (const_row, k, 0, 0))


def _ada_kernel(c_ref, w_ref, b_ref, o_ref):
    s = _silu(c_ref[...]).astype(BF16)
    o_ref[...] = jnp.dot(s, w_ref[...].astype(BF16), preferred_element_type=F32) + b_ref[...]


def _ada(cc, ada_w, ada_b):
    depth, d, n = ada_w.shape
    tn = _pick(n, 1024)
    return pl.pallas_call(
        _ada_kernel,
        grid=(depth, n // tn),
        in_specs=[
            pl.BlockSpec((8, d), lambda l, j: (0, 0)),
            pl.BlockSpec((None, d, tn), lambda l, j: (l, 0, j)),
            pl.BlockSpec((None, 1, tn), lambda l, j: (l, 0, j)),
        ],
        out_specs=pl.BlockSpec((None, 8, tn), lambda l, j: (l, 0, j)),
        out_shape=jax.ShapeDtypeStruct((depth, 8, n), F32),
        compiler_params=_params("arbitrary", "arbitrary"),
        name="ada_mod",
    )(cc, ada_w, ada_b.reshape(depth, 1, n))


def _mla_down_kernel(x_ref, g_ref, sh_ref, sc_ref, w_ref, qn_ref, kvn_ref, cos_ref, sin_ref,
                     cq_ref, ckv_ref, kpe_ref, *, qr, kvr):
    h = _norm_mod(x_ref[...], g_ref[...], sh_ref[...], sc_ref[...]).astype(BF16)
    a = jnp.dot(h, w_ref[...], preferred_element_type=F32)
    cq_ref[...] = (_rms(a[:, :qr]) * qn_ref[...]).astype(BF16)
    ckv_ref[...] = (_rms(a[:, qr:qr + kvr]) * kvn_ref[...]).astype(BF16)
    pe = a[:, qr + kvr:qr + kvr + LANES]
    pe_swapped = a[:, qr + kvr + LANES:]
    kpe_ref[...] = (pe * cos_ref[...] + pe_swapped * sin_ref[...]).astype(BF16)


def _mla_down(x, gain, modl, w, qn, kvn, cos, sin, *, rows_per_mod, const_row, qr, kvr):
    n, d = x.shape
    tm = _pick(n, 512)
    tiles_per_row = None if rows_per_mod is None else rows_per_mod // tm
    ntab = cos.shape[0] // tm
    vec = lambda m: pl.BlockSpec((1, m), lambda i: (0, 0))
    tab = pl.BlockSpec((tm, LANES), lambda i: (i % ntab, 0))
    row = lambda m: pl.BlockSpec((tm, m), lambda i: (i, 0))
    return pl.pallas_call(
        functools.partial(_mla_down_kernel, qr=qr, kvr=kvr),
        grid=(n // tm,),
        in_specs=[row(d), vec(d), _mod_spec(0, d, tiles_per_row, const_row),
                  _mod_spec(1, d, tiles_per_row, const_row),
                  pl.BlockSpec(w.shape, lambda i: (0, 0)), vec(qr), vec(kvr), tab, tab],
        out_specs=[row(qr), row(kvr), row(LANES)],
        out_shape=[jax.ShapeDtypeStruct((n, qr), BF16), jax.ShapeDtypeStruct((n, kvr), BF16),
                   jax.ShapeDtypeStruct((n, LANES), BF16)],
        compiler_params=_params("arbitrary"),
        name="mla_down",
    )(x, gain, modl, modl, w, qn, kvn, cos, sin)


def _mla_qup_kernel(cq_ref, w_ref, cos_ref, sin_ref, q_ref):
    r = jnp.dot(cq_ref[...], w_ref[...], preferred_element_type=F32)
    r_pe = r[:, MLA_NOPE:]
    pe = r_pe * cos_ref[...] + pltpu.roll(r_pe, MLA_ROPE, 1) * sin_ref[...]
    q_ref[:, :MLA_NOPE] = r[:, :MLA_NOPE].astype(BF16)
    q_ref[:, MLA_NOPE:] = pe.astype(BF16)


def _mla_qup(cq, wq, cos, sin):
    n, qr = cq.shape
    heads, _, qw = wq.shape
    tm = _pick(n, 512)
    ntab = cos.shape[0] // tm
    return pl.pallas_call(
        _mla_qup_kernel,
        grid=(n // tm, heads),
        in_specs=[pl.BlockSpec((tm, qr), lambda i, h: (i, 0)),
                  pl.BlockSpec((None, qr, qw), lambda i, h: (h, 0, 0)),
                  pl.BlockSpec((tm, LANES), lambda i, h: (i % ntab, 0)),
                  pl.BlockSpec((tm, LANES), lambda i, h: (i % ntab, 0))],
        out_specs=pl.BlockSpec((tm, qw), lambda i, h: (i, h)),
        out_shape=jax.ShapeDtypeStruct((n, heads * qw), BF16),
        compiler_params=_params("arbitrary", "arbitrary"),
        name="mla_qup",
    )(cq, wq, cos, sin)


def _mla_kvup_kernel(ckv_ref, wk_ref, wvt_ref, kpe_ref, k_ref, vt_ref):
    c = ckv_ref[...]
    k_ref[:, :MLA_NOPE] = jnp.dot(c, wk_ref[...], preferred_element_type=F32).astype(BF16)
    k_ref[:, MLA_NOPE:] = kpe_ref[...]
    vt_ref[...] = lax.dot_general(wvt_ref[...], c, NT_DIMS, preferred_element_type=F32).astype(BF16)


def _mla_kvup(ckv, kpe, wk, wvt, *, batch):
    n, kvr = ckv.shape
    heads = wk.shape[0]
    per_batch = n // batch
    tm = _pick(per_batch, 512)
    tpb = per_batch // tm
    kw = MLA_NOPE + LANES
    return pl.pallas_call(
        _mla_kvup_kernel,
        grid=(n // tm, heads),
        in_specs=[pl.BlockSpec((tm, kvr), lambda i, h: (i, 0)),
                  pl.BlockSpec((None, kvr, MLA_NOPE), lambda i, h: (h, 0, 0)),
                  pl.BlockSpec((None, MLA_V, kvr), lambda i, h: (h, 0, 0)),
                  pl.BlockSpec((tm, LANES), lambda i, h: (i, 0))],
        out_specs=[pl.BlockSpec((tm, kw), lambda i, h: (i, h)),
                   pl.BlockSpec((None, None, MLA_V, tm), lambda i, h: (i // tpb, h, 0, i % tpb))],
        out_shape=[jax.ShapeDtypeStruct((n, heads * kw), BF16),
                   jax.ShapeDtypeStruct((batch, heads, MLA_V, per_batch), BF16)],
        compiler_params=_params("arbitrary", "arbitrary"),
        name="mla_kvup",
    )(ckv, wk, wvt, kpe)


def _flash_t(q, key_chunks):
    tq = q.shape[0]
    m = jnp.full((1, tq), -jnp.inf, F32)
    l = jnp.zeros((1, tq), F32)
    acc = jnp.zeros((MLA_V, tq), F32)
    for k_ref, vt_ref, c0, cs in key_chunks:
        s = lax.dot_general(k_ref[pl.ds(c0, cs), :], q, NT_DIMS, preferred_element_type=F32)
        m_new = jnp.maximum(m, jnp.max(s, axis=0, keepdims=True))
        alpha = jnp.exp(m - m_new)
        p = jnp.exp(s - m_new)
        l = alpha * l + jnp.sum(p, axis=0, keepdims=True)
        acc = alpha * acc + jnp.dot(vt_ref[:, pl.ds(c0, cs)], p.astype(BF16),
                                    preferred_element_type=F32)
        m = m_new
    return acc * (1.0 / l)


def _mla_attn_kernel(q_ref, qc_ref, k_ref, kc_ref, vt_ref, vtc_ref, o_ref, oc_ref, *, tq, tk):
    s_len, lc = k_ref.shape[0], kc_ref.shape[0]
    chunks = [(k_ref, vt_ref, c0, tk) for c0 in range(0, s_len, tk)] + [(kc_ref, vtc_ref, 0, lc)]

    def q_tile(i, carry):
        r0 = pl.multiple_of(i * tq, tq)
        o_ref[pl.ds(r0, tq), :] = _flash_t(q_ref[pl.ds(r0, tq), :], chunks).T.astype(BF16)
        return carry

    lax.fori_loop(0, s_len // tq, q_tile, 0)
    oc_ref[...] = _flash_t(qc_ref[...], [(kc_ref, vtc_ref, 0, lc)]).T.astype(BF16)


def _mla_attn(q, qc, k, kc, vt, vtc, *, batch, heads):
    s_len = q.shape[0] // batch
    lc = qc.shape[0] // batch
    qw = q.shape[1] // heads
    tq = _pick(s_len, 512)
    tk = _pick(s_len, 1024)
    blk = lambda rows, w: pl.BlockSpec((rows, w), lambda b, h: (b, h))
    blk_t = lambda cols: pl.BlockSpec((None, None, MLA_V, cols), lambda b, h: (b, h, 0, 0))
    return pl.pallas_call(
        functools.partial(_mla_attn_kernel, tq=tq, tk=tk),
        grid=(batch, heads),
        in_specs=[blk(s_len, qw), blk(lc, qw), blk(s_len, qw), blk(lc, qw), blk_t(s_len), blk_t(lc)],
        out_specs=[blk(s_len, MLA_V), blk(lc, MLA_V)],
        out_shape=[jax.ShapeDtypeStruct((batch * s_len, heads * MLA_V), BF16),
                   jax.ShapeDtypeStruct((batch * lc, heads * MLA_V), BF16)],
        compiler_params=_params("arbitrary", "arbitrary"),
        name="mla_attn",
    )(q, qc, k, kc, vt, vtc)


def _proj_res_kernel(o_ref, w_ref, x_ref, gate_ref, out_ref):
    y = jnp.dot(o_ref[...], w_ref[...], preferred_element_type=F32)
    out_ref[...] = x_ref[...] + gate_ref[...] * y


def _proj_res(o, w, x, modl, *, rows_per_mod, const_row):
    n, d = x.shape
    kdim = o.shape[1]
    tm = _pick(n, 512)
    tiles_per_row = None if rows_per_mod is None else rows_per_mod // tm
    return pl.pallas_call(
        _proj_res_kernel,
        grid=(n // tm,),
        in_specs=[pl.BlockSpec((tm, kdim), lambda i: (i, 0)),
                  pl.BlockSpec((kdim, d), lambda i: (0, 0)),
                  pl.BlockSpec((tm, d), lambda i: (i, 0)),
                  _mod_spec(2, d, tiles_per_row, const_row)],
        out_specs=pl.BlockSpec((tm, d), lambda i: (i, 0)),
        out_shape=jax.ShapeDtypeStruct((n, d), F32),
        compiler_params=_params("arbitrary"),
        name="proj_res",
    )(o, w, x, modl)


def _ffn_kernel(x_ref, g_ref, sh_ref, sc_ref, gate_ref, fg_ref, wg_ref, wu_ref, wd_ref, o_ref, h_scr,
                *, final_norm):
    j = pl.program_id(1)

    @pl.when(j == 0)
    def _():
        h_scr[...] = _norm_mod(x_ref[...], g_ref[...], sh_ref[...], sc_ref[...]).astype(BF16)
        o_ref[...] = jnp.zeros_like(o_ref)

    h = h_scr[...]
    a = jnp.dot(h, wg_ref[...], preferred_element_type=F32)
    u = jnp.dot(h, wu_ref[...], preferred_element_type=F32)
    act = (_silu(a) * u).astype(BF16)
    o_ref[...] += jnp.dot(act, wd_ref[...], preferred_element_type=F32)

    @pl.when(j == pl.num_programs(1) - 1)
    def _():
        y = x_ref[...] + gate_ref[...] * o_ref[...]
        if final_norm:
            y = _rms(y) * fg_ref[...]
        o_ref[...] = y


def _ffn(x, gain, modl, final_gain, wg, wu, wd, *, rows_per_mod, const_row, final_norm):
    n, d = x.shape
    dff = wg.shape[1]
    tm = _pick(n, 512)
    tf = _pick(dff, 512)
    tiles_per_row = None if rows_per_mod is None else rows_per_mod // tm
    vec = pl.BlockSpec((1, d), lambda i, j: (0, 0))
    ms = lambda k: _mod_spec(k, d, tiles_per_row, const_row)
    return pl.pallas_call(
        functools.partial(_ffn_kernel, final_norm=final_norm),
        grid=(n // tm, dff // tf),
        in_specs=[pl.BlockSpec((tm, d), lambda i, j: (i, 0)), vec, ms(3), ms(4), ms(5), vec,
                  pl.BlockSpec((d, tf), lambda i, j: (0, j)),
                  pl.BlockSpec((d, tf), lambda i, j: (0, j)),
                  pl.BlockSpec((tf, d), lambda i, j: (j, 0))],
        out_specs=pl.BlockSpec((tm, d), lambda i, j: (i, 0)),
        out_shape=jax.ShapeDtypeStruct((n, d), F32),
        scratch_shapes=[pltpu.VMEM((tm, d), BF16)],
        compiler_params=_params("arbitrary", "arbitrary"),
        name="ffn",
    )(x, gain, modl, modl, modl, final_gain, wg, wu, wd)


def _norm_matmul_kernel(x_ref, g_ref, sh_ref, sc_ref, w_ref, o_ref, h_scr):
    @pl.when(pl.program_id(1) == 0)
    def _():
        h_scr[...] = _norm_mod(x_ref[...], g_ref[...], sh_ref[...], sc_ref[...]).astype(BF16)

    o_ref[...] = jnp.dot(h_scr[...], w_ref[...], preferred_element_type=F32).astype(BF16)


def _norm_matmul(x, gain, modl, w, *, rows_per_mod, const_row):
    n, d = x.shape
    nout = w.shape[1]
    tm = _pick(n, 512)
    tn = _pick(nout, 1024)
    tiles_per_row = None if rows_per_mod is None else rows_per_mod // tm
    return pl.pallas_call(
        _norm_matmul_kernel,
        grid=(n // tm, nout // tn),
        in_specs=[pl.BlockSpec((tm, d), lambda i, j: (i, 0)),
                  pl.BlockSpec((1, d), lambda i, j: (0, 0)),
                  _mod_spec(0, d, tiles_per_row, const_row), _mod_spec(1, d, tiles_per_row, const_row),
                  pl.BlockSpec((d, tn), lambda i, j: (0, j))],
        out_specs=pl.BlockSpec((tm, tn), lambda i, j: (i, j)),
        out_shape=jax.ShapeDtypeStruct((n, nout), BF16),
        scratch_shapes=[pltpu.VMEM((tm, d), BF16)],
        compiler_params=_params("arbitrary", "arbitrary"),
        name="norm_matmul",
    )(x, gain, modl, modl, w)


def _na_group_starts(rows, win_r):
    slab = NA_ROW_GROUP + win_r
    return [min(max(g * NA_ROW_GROUP - win_r // 2, 0), rows - slab) for g in range(rows // NA_ROW_GROUP)]


def _na_bias_tables(rel_bias, rows, win_r, win_c):
    groups = rows // NA_ROW_GROUP
    slab = NA_ROW_GROUP + win_r
    starts = _na_group_starts(rows, win_r)
    tables = []
    for g in (0, min(1, groups - 1), groups - 1):
        qr = g * NA_ROW_GROUP + jnp.arange(NA_ROW_GROUP)
        kr = starts[g] + jnp.arange(slab)
        rs = jnp.clip(qr - win_r // 2, 0, rows - win_r)
        ok_r = (kr[:, None] >= rs[None, :]) & (kr[:, None] < rs[None, :] + win_r)
        dr = jnp.clip(kr[:, None] - qr[None, :] + (win_r - 1), 0, 2 * win_r - 2)
        qc = jnp.arange(GRID_W)
        kc = jnp.arange(GRID_W)
        cs = jnp.clip(qc - win_c // 2, 0, GRID_W - win_c)
        ok_c = (kc[:, None] >= cs[None, :]) & (kc[:, None] < cs[None, :] + win_c)
        dc = jnp.clip(kc[:, None] - qc[None, :] + (win_c - 1), 0, 2 * win_c - 2)
        b = rel_bias[:, dr[:, None, :, None], dc[None, :, None, :]]
        ok = ok_r[:, None, :, None] & ok_c[None, :, None, :]
        b = jnp.where(ok[None], b, MASK_VALUE)
        tables.append(b.reshape(rel_bias.shape[0], slab * GRID_W, NA_ROW_GROUP * GRID_W))
    return jnp.stack(tables, axis=1).astype(F32)


def _na_attn_kernel(q_ref, k_ref, v_ref, kc_ref, vc_ref, bias_ref, o_ref, *, starts):
    gq = NA_ROW_GROUP * GRID_W
    nk = bias_ref.shape[1]
    kc = kc_ref[...]
    vc = vc_ref[...]
    last = len(starts) - 1
    for g, start in enumerate(starts):
        q = q_ref[g * gq:(g + 1) * gq, :]
        k0 = start * GRID_W
        table = 0 if g == 0 else (2 if g == last else 1)
        s_loc = lax.dot_general(k_ref[k0:k0 + nk, :], q, NT_DIMS, preferred_element_type=F32)
        s_loc = s_loc + bias_ref[table]
        s_ctx = lax.dot_general(kc, q, NT_DIMS, preferred_element_type=F32)
        m = jnp.maximum(jnp.max(s_loc, axis=0, keepdims=True), jnp.max(s_ctx, axis=0, keepdims=True))
        p_loc = jnp.exp(s_loc - m)
        p_ctx = jnp.exp(s_ctx - m)
        l = jnp.sum(p_loc, axis=0, keepdims=True) + jnp.sum(p_ctx, axis=0, keepdims=True)
        o_t = lax.dot_general(v_ref[k0:k0 + nk, :], p_loc.astype(BF16), TN_DIMS, preferred_element_type=F32)
        o_t = o_t + lax.dot_general(vc, p_ctx.astype(BF16), TN_DIMS, preferred_element_type=F32)
        o_ref[g * gq:(g + 1) * gq, :] = (o_t * (1.0 / l)).T.astype(BF16)


def _na_attn(qkv, kvc, bias, *, batch, heads, rows, win_r):
    s_len = qkv.shape[0] // batch
    lc = kvc.shape[0] // batch
    dh = qkv.shape[1] // (3 * heads)
    nk, gq = bias.shape[2], bias.shape[3]
    starts = _na_group_starts(rows, win_r)
    return pl.pallas_call(
        functools.partial(_na_attn_kernel, starts=starts),
        grid=(heads, batch),
        in_specs=[pl.BlockSpec((s_len, dh), lambda h, b: (b, h)),
                  pl.BlockSpec((s_len, dh), lambda h, b: (b, heads + h)),
                  pl.BlockSpec((s_len, dh), lambda h, b: (b, 2 * heads + h)),
                  pl.BlockSpec((lc, dh), lambda h, b: (b, h)),
                  pl.BlockSpec((lc, dh), lambda h, b: (b, heads + h)),
                  pl.BlockSpec((None, 3, nk, gq), lambda h, b: (h, 0, 0, 0))],
        out_specs=pl.BlockSpec((s_len, dh), lambda h, b: (b, h)),
        out_shape=jax.ShapeDtypeStruct((batch * s_len, heads * dh), BF16),
        compiler_params=_params("arbitrary", "arbitrary"),
        name="na_attn",
    )(qkv, qkv, qkv, kvc, kvc, bias)


def _rope_tables(s_len):
    t = jnp.arange(s_len)
    row = (t // GRID_W).astype(F32)
    col = (t % GRID_W).astype(F32)
    n_freq = MLA_ROPE // 4
    inv_freq = ROPE_THETA ** (-jnp.arange(n_freq, dtype=F32) / n_freq)
    ang_r = row[:, None] * inv_freq
    ang_c = col[:, None] * inv_freq
    cos = jnp.concatenate([jnp.cos(ang_r), jnp.cos(ang_r), jnp.cos(ang_c), jnp.cos(ang_c)], axis=-1)
    sin = jnp.concatenate([-jnp.sin(ang_r), jnp.sin(ang_r), -jnp.sin(ang_c), jnp.sin(ang_c)], axis=-1)
    pad = jnp.zeros((s_len, LANES - MLA_ROPE), F32)
    return jnp.concatenate([cos, pad], axis=-1), jnp.concatenate([sin, pad], axis=-1)


def _rope_swap_perm():
    q = MLA_ROPE // 4
    return jnp.concatenate([jnp.arange(q, 2 * q), jnp.arange(0, q), jnp.arange(3 * q, 4 * q), jnp.arange(2 * q, 3 * q)])


def _mla_layer(xl, xc, modl, gain, w_dq, q_norm, w_uq, w_dkv, kv_norm, w_ukv, w_o, rope, *, batch, s_len, with_ctx_out):
    d = xl.shape[1]
    qr = w_dq.shape[1]
    kvr = kv_norm.shape[0]
    heads = w_uq.shape[1] // (MLA_NOPE + MLA_ROPE)
    scale = (MLA_NOPE + MLA_ROPE) ** -0.5
    perm = _rope_swap_perm()
    pad = jnp.zeros((d, LANES - MLA_ROPE), F32)
    w_pe = w_dkv[:, kvr:]
    w_down = jnp.concatenate([w_dq, w_dkv[:, :kvr], w_pe, pad, w_pe[:, perm], pad], axis=1).astype(BF16)
    wq = w_uq.reshape(qr, heads, MLA_NOPE + MLA_ROPE)
    wq = jnp.concatenate([wq, wq[:, :, MLA_NOPE:][:, :, perm]], axis=-1) * scale
    wq = wq.transpose(1, 0, 2).astype(BF16)
    wkv = w_ukv.reshape(kvr, heads, MLA_NOPE + MLA_V)
    wk = wkv[:, :, :MLA_NOPE].transpose(1, 0, 2).astype(BF16)
    wvt = wkv[:, :, MLA_NOPE:].transpose(1, 2, 0).astype(BF16)
    qn = q_norm.reshape(1, qr)
    kvn = kv_norm.reshape(1, kvr)
    cos, sin = rope
    nc = xc.shape[0]
    cos_c = jnp.concatenate([jnp.ones((nc, MLA_ROPE), F32), jnp.zeros((nc, LANES - MLA_ROPE), F32)], axis=-1)
    sin_c = jnp.zeros((nc, LANES), F32)

    cq, ckv, kpe = _mla_down(xl, gain, modl, w_down, qn, kvn, cos, sin,
                             rows_per_mod=s_len, const_row=None, qr=qr, kvr=kvr)
    cqc, ckvc, kpec = _mla_down(xc, gain, modl, w_down, qn, kvn, cos_c, sin_c,
                                rows_per_mod=None, const_row=batch, qr=qr, kvr=kvr)
    q = _mla_qup(cq, wq, cos, sin)
    qc = _mla_qup(cqc, wq, cos_c, sin_c)
    k, vt = _mla_kvup(ckv, kpe, wk, wvt, batch=batch)
    kc, vtc = _mla_kvup(ckvc, kpec, wk, wvt, batch=batch)
    o, oc = _mla_attn(q, qc, k, kc, vt, vtc, batch=batch, heads=heads)
    w_o = w_o.astype(BF16)
    xl = _proj_res(o, w_o, xl, modl, rows_per_mod=s_len, const_row=None)
    if with_ctx_out:
        xc = _proj_res(oc, w_o, xc, modl, rows_per_mod=None, const_row=batch)
    return xl, xc


def _na_layer(xl, xc, modl, gain, w_qkv, rel_bias, w_o, *, batch, s_len, with_ctx_out):
    if with_ctx_out:
        raise NotImplementedError("neighbourhood-attention layers with a context output")
    d = xl.shape[1]
    heads = rel_bias.shape[0]
    dh = d // heads
    hd = heads * dh
    win_r = (rel_bias.shape[1] + 1) // 2
    win_c = (rel_bias.shape[2] + 1) // 2
    rows = s_len // GRID_W
    assert dh == LANES and rows % NA_ROW_GROUP == 0 and rows >= NA_ROW_GROUP + win_r
    scale = dh ** -0.5
    w = jnp.concatenate([w_qkv[:, :hd] * scale, w_qkv[:, hd:]], axis=1).astype(BF16)
    qkv = _norm_matmul(xl, gain, modl, w, rows_per_mod=s_len, const_row=None)
    kvc = _norm_matmul(xc, gain, modl, w[:, hd:], rows_per_mod=None, const_row=batch)
    bias = _na_bias_tables(rel_bias, rows, win_r, win_c)
    o = _na_attn(qkv, kvc, bias, batch=batch, heads=heads, rows=rows, win_r=win_r)
    xl = _proj_res(o, w_o.astype(BF16), xl, modl, rows_per_mod=s_len, const_row=None)
    return xl, xc


def kernel(x, c, ctx, c_ctx, ada_w, ada_b, norm_mix, norm_ffn, norm_final, mla_w_dq, mla_q_norm, mla_w_uq, mla_w_dkv, mla_kv_norm, mla_w_ukv, mla_w_o, na_w_qkv, na_rel_bias, na_w_o, ffn_w_gate, ffn_w_up, ffn_w_down):
    batch, s_len, d = x.shape
    lc = ctx.shape[1]
    depth = ada_w.shape[0]
    assert batch < 8
    cc = jnp.concatenate([c, c_ctx[None, :], jnp.zeros((8 - batch - 1, d), F32)], axis=0)
    mod = _ada(cc, ada_w, ada_b).reshape(depth, 8, 6, 1, d)
    rope = _rope_tables(s_len)
    xl = x.reshape(batch * s_len, d)
    xc = ctx.reshape(batch * lc, d)
    final_gain = norm_final.reshape(1, d)
    for i in range(depth):
        last = i == depth - 1
        j = i // N_MIXERS
        modl = mod[i]
        gain = norm_mix[i].reshape(1, d)
        if i % N_MIXERS == 0:
            xl, xc = _mla_layer(xl, xc, modl, gain, mla_w_dq[j], mla_q_norm[j], mla_w_uq[j], mla_w_dkv[j],
                                mla_kv_norm[j], mla_w_ukv[j], mla_w_o[j], rope,
                                batch=batch, s_len=s_len, with_ctx_out=not last)
        else:
            xl, xc = _na_layer(xl, xc, modl, gain, na_w_qkv[j], na_rel_bias[j], na_w_o[j],
                               batch=batch, s_len=s_len, with_ctx_out=not last)
        fgain = norm_ffn[i].reshape(1, d)
        wg, wu, wd = ffn_w_gate[i].astype(BF16), ffn_w_up[i].astype(BF16), ffn_w_down[i].astype(BF16)
        xl = _ffn(xl, fgain, modl, final_gain, wg, wu, wd, rows_per_mod=s_len, const_row=None, final_norm=last)
        if not last:
            xc = _ffn(xc, fgain, modl, final_gain, wg, wu, wd, rows_per_mod=None, const_row=batch, final_norm=False)
    return xl.reshape(batch, s_len, d)
```

```python
import functools

import jax
import jax.numpy as jnp
import numpy as np
from jax import lax
from jax.experimental import pallas as pl
from jax.experimental.pallas import tpu as pltpu

GRID_W = 64
N_MIXERS = 2
MLA_NOPE = 128
MLA_ROPE = 64
MLA_V = 128
ROPE_THETA = 10000.0
EPS = 1e-6
NA_ROW_GROUP = 8
MASK_VALUE = -1e30
LOG2E = 1.4426950408889634

LANES = 128
V7X_VMEM_BYTES = 64 * 1024 * 1024
VMEM_LIMIT = V7X_VMEM_BYTES * 3 // 4

F32 = jnp.float32
BF16 = jnp.bfloat16
NT_DIMS = (((1,), (1,)), ((), ()))
TN_DIMS = (((0,), (0,)), ((), ()))


def _pick(n, pref):
    t = min(n, pref)
    while n % t:
        t //= 2
    return t


def _params(*sem):
    return pltpu.CompilerParams(dimension_semantics=sem, vmem_limit_bytes=VMEM_LIMIT)


def _rms(x):
    return x * lax.rsqrt(jnp.mean(x * x, axis=-1, keepdims=True) + EPS)


def _norm_mod(x, g, sh, sc):
    return (_rms(x) * g) * (1.0 + sc) + sh


def _silu(x):
    return x * (1.0 / (1.0 + jnp.exp(-x)))


def _mod_spec(k, d, tiles_per_row, const_row):
    if const_row is None:
        return pl.BlockSpec((None, None, 1, d), lambda i, *_: (i // tiles_per_row, k, 0, 0))
    return pl.BlockSpec((None, None, 1, d), lambda i, *_: (const_row, k, 0, 0))


def _ada_kernel(c_ref, w_ref, b_ref, o_ref):
    s = _silu(c_ref[...]).astype(BF16)
    o_ref[...] = jnp.dot(s, w_ref[...].astype(BF16), preferred_element_type=F32) + b_ref[...]


def _ada(cc, ada_w, ada_b):
    depth, d, n = ada_w.shape
    tn = _pick(n, 1024)
    return pl.pallas_call(
        _ada_kernel,
        grid=(depth, n // tn),
        in_specs=[
            pl.BlockSpec((8, d), lambda l, j: (0, 0)),
            pl.BlockSpec((None, d, tn), lambda l, j: (l, 0, j)),
            pl.BlockSpec((None, 1, tn), lambda l, j: (l, 0, j)),
        ],
        out_specs=pl.BlockSpec((None, 8, tn), lambda l, j: (l, 0, j)),
        out_shape=jax.ShapeDtypeStruct((depth, 8, n), F32),
        compiler_params=_params("arbitrary", "arbitrary"),
        name="ada_mod",
    )(cc, ada_w, ada_b.reshape(depth, 1, n))


def _mla_down_kernel(x_ref, g_ref, sh_ref, sc_ref, w_ref, qn_ref, kvn_ref, cos_ref, sin_ref,
                     cq_ref, ckv_ref, kpe_ref, *, qr, kvr):
    h = _norm_mod(x_ref[...], g_ref[...], sh_ref[...], sc_ref[...]).astype(BF16)
    a = jnp.dot(h, w_ref[...], preferred_element_type=F32)
    cq_ref[...] = (_rms(a[:, :qr]) * qn_ref[...]).astype(BF16)
    ckv_ref[...] = (_rms(a[:, qr:qr + kvr]) * kvn_ref[...]).astype(BF16)
    pe = a[:, qr + kvr:qr + kvr + LANES]
    pe_swapped = a[:, qr + kvr + LANES:]
    kpe_ref[...] = (pe * cos_ref[...] + pe_swapped * sin_ref[...]).astype(BF16)


def _mla_down(x, gain, modl, w, qn, kvn, cos, sin, *, rows_per_mod, const_row, qr, kvr):
    n, d = x.shape
    tm = _pick(n, 512)
    tiles_per_row = None if rows_per_mod is None else rows_per_mod // tm
    ntab = cos.shape[0] // tm
    vec = lambda m: pl.BlockSpec((1, m), lambda i: (0, 0))
    tab = pl.BlockSpec((tm, LANES), lambda i: (i % ntab, 0))
    row = lambda m: pl.BlockSpec((tm, m), lambda i: (i, 0))
    return pl.pallas_call(
        functools.partial(_mla_down_kernel, qr=qr, kvr=kvr),
        grid=(n // tm,),
        in_specs=[row(d), vec(d), _mod_spec(0, d, tiles_per_row, const_row),
                  _mod_spec(1, d, tiles_per_row, const_row),
                  pl.BlockSpec(w.shape, lambda i: (0, 0)), vec(qr), vec(kvr), tab, tab],
        out_specs=[row(qr), row(kvr), row(LANES)],
        out_shape=[jax.ShapeDtypeStruct((n, qr), BF16), jax.ShapeDtypeStruct((n, kvr), BF16),
                   jax.ShapeDtypeStruct((n, LANES), BF16)],
        compiler_params=_params("arbitrary"),
        name="mla_down",
    )(x, gain, modl, modl, w, qn, kvn, cos, sin)


def _mla_qup_kernel(cq_ref, w_ref, cos_ref, sin_ref, q_ref, *, heads):
    cq = cq_ref[...]
    cos = cos_ref[...]
    sin = sin_ref[...]
    qw = MLA_NOPE + LANES
    for h in range(heads):
        r = jnp.dot(cq, w_ref[:, h * qw:(h + 1) * qw], preferred_element_type=F32)
        r_pe = r[:, MLA_NOPE:]
        pe = r_pe * cos + pltpu.roll(r_pe, MLA_ROPE, 1) * sin
        q_ref[:, h * qw:h * qw + MLA_NOPE] = r[:, :MLA_NOPE].astype(BF16)
        q_ref[:, h * qw + MLA_NOPE:(h + 1) * qw] = pe.astype(BF16)


def _mla_qup(cq, wq, cos, sin, *, heads):
    n, qr = cq.shape
    nq = wq.shape[1]
    tm = _pick(n, 512)
    ntab = cos.shape[0] // tm
    return pl.pallas_call(
        functools.partial(_mla_qup_kernel, heads=heads),
        grid=(n // tm,),
        in_specs=[pl.BlockSpec((tm, qr), lambda i: (i, 0)),
                  pl.BlockSpec((qr, nq), lambda i: (0, 0)),
                  pl.BlockSpec((tm, LANES), lambda i: (i % ntab, 0)),
                  pl.BlockSpec((tm, LANES), lambda i: (i % ntab, 0))],
        out_specs=pl.BlockSpec((tm, nq), lambda i: (i, 0)),
        out_shape=jax.ShapeDtypeStruct((n, nq), BF16),
        compiler_params=_params("arbitrary"),
        name="mla_qup",
    )(cq, wq, cos, sin)


def _mla_kvup_kernel(ckv_ref, wk_ref, wvt_ref, kpe_ref, k_ref, vt_ref, *, heads):
    c = ckv_ref[...]
    kpe = kpe_ref[...]
    kw = MLA_NOPE + LANES
    k_nope = jnp.dot(c, wk_ref[...], preferred_element_type=F32).astype(BF16)
    for h in range(heads):
        k_ref[:, h * kw:h * kw + MLA_NOPE] = k_nope[:, h * MLA_NOPE:(h + 1) * MLA_NOPE]
        k_ref[:, h * kw + MLA_NOPE:(h + 1) * kw] = kpe
    vt = lax.dot_general(wvt_ref[...], c, NT_DIMS, preferred_element_type=F32)
    vt_ref[...] = vt.astype(BF16).reshape(vt_ref.shape)


def _mla_kvup(ckv, kpe, wk, wvt, *, batch, heads):
    n, kvr = ckv.shape
    per_batch = n // batch
    tm = _pick(per_batch, 512)
    tpb = per_batch // tm
    kw = MLA_NOPE + LANES
    return pl.pallas_call(
        functools.partial(_mla_kvup_kernel, heads=heads),
        grid=(n // tm,),
        in_specs=[pl.BlockSpec((tm, kvr), lambda i: (i, 0)),
                  pl.BlockSpec(wk.shape, lambda i: (0, 0)),
                  pl.BlockSpec(wvt.shape, lambda i: (0, 0)),
                  pl.BlockSpec((tm, LANES), lambda i: (i, 0))],
        out_specs=[pl.BlockSpec((tm, heads * kw), lambda i: (i, 0)),
                   pl.BlockSpec((None, heads, MLA_V, tm), lambda i: (i // tpb, 0, 0, i % tpb))],
        out_shape=[jax.ShapeDtypeStruct((n, heads * kw), BF16),
                   jax.ShapeDtypeStruct((batch, heads, MLA_V, per_batch), BF16)],
        compiler_params=_params("arbitrary"),
        name="mla_kvup",
    )(ckv, wk, wvt, kpe)


def _scores_t(chunk, q):
    k_ref, _, c0, cs = chunk
    return lax.dot_general(k_ref[pl.ds(c0, cs), :], q, NT_DIMS, preferred_element_type=F32)


def _flash_t(qs, key_chunks):
    tq = qs[0].shape[0]
    m = [jnp.full((1, tq), -jnp.inf, F32) for _ in qs]
    l = [jnp.zeros((1, tq), F32) for _ in qs]
    acc = [jnp.zeros((MLA_V, tq), F32) for _ in qs]
    s_next = [_scores_t(key_chunks[0], q) for q in qs]
    for idx, (_, vt_ref, c0, cs) in enumerate(key_chunks):
        for t, q in enumerate(qs):
            s = s_next[t]
            if idx + 1 < len(key_chunks):
                s_next[t] = _scores_t(key_chunks[idx + 1], q)
            m_new = jnp.maximum(m[t], jnp.max(s, axis=0, keepdims=True))
            alpha = jnp.exp2(m[t] - m_new)
            p = jnp.exp2(s - m_new)
            l[t] = alpha * l[t] + jnp.sum(p, axis=0, keepdims=True)
            acc[t] = alpha * acc[t] + jnp.dot(vt_ref[:, pl.ds(c0, cs)], p.astype(BF16),
                                              preferred_element_type=F32)
            m[t] = m_new
    return [a * (1.0 / d) for a, d in zip(acc, l)]


def _mla_attn_kernel(q_ref, qc_ref, k_ref, kc_ref, vt_ref, vtc_ref, o_ref, oc_ref, *, tq, tk, lockstep):
    s_len, lc = k_ref.shape[0], kc_ref.shape[0]
    chunks = [(k_ref, vt_ref, c0, tk) for c0 in range(0, s_len, tk)] + [(kc_ref, vtc_ref, 0, lc)]

    def q_tiles(i, carry):
        rows = [pl.ds(pl.multiple_of((i * lockstep + t) * tq, tq), tq) for t in range(lockstep)]
        for r, o_t in zip(rows, _flash_t([q_ref[r, :] for r in rows], chunks)):
            o_ref[r, :] = o_t.T.astype(BF16)
        return carry

    lax.fori_loop(0, s_len // (tq * lockstep), q_tiles, 0)
    oc_ref[...] = _flash_t([qc_ref[...]], [(kc_ref, vtc_ref, 0, lc)])[0].T.astype(BF16)


def _mla_attn(q, qc, k, kc, vt, vtc, *, batch, heads):
    s_len = q.shape[0] // batch
    lc = qc.shape[0] // batch
    qw = q.shape[1] // heads
    tq = _pick(s_len, 512)
    tk = _pick(s_len, 1024)
    lockstep = 2 if s_len % (2 * tq) == 0 else 1
    blk = lambda rows, w: pl.BlockSpec((rows, w), lambda b, h: (b, h))
    blk_t = lambda cols: pl.BlockSpec((None, None, MLA_V, cols), lambda b, h: (b, h, 0, 0))
    return pl.pallas_call(
        functools.partial(_mla_attn_kernel, tq=tq, tk=tk, lockstep=lockstep),
        grid=(batch, heads),
        in_specs=[blk(s_len, qw), blk(lc, qw), blk(s_len, qw), blk(lc, qw), blk_t(s_len), blk_t(lc)],
        out_specs=[blk(s_len, MLA_V), blk(lc, MLA_V)],
        out_shape=[jax.ShapeDtypeStruct((batch * s_len, heads * MLA_V), BF16),
                   jax.ShapeDtypeStruct((batch * lc, heads * MLA_V), BF16)],
        compiler_params=_params("arbitrary", "arbitrary"),
        name="mla_attn",
    )(q, qc, k, kc, vt, vtc)


def _proj_res_kernel(o_ref, w_ref, x_ref, gate_ref, out_ref):
    y = jnp.dot(o_ref[...], w_ref[...], preferred_element_type=F32)
    out_ref[...] = x_ref[...] + gate_ref[...] * y


def _proj_res(o, w, x, modl, *, rows_per_mod, const_row):
    n, d = x.shape
    kdim = o.shape[1]
    tm = _pick(n, 512)
    tiles_per_row = None if rows_per_mod is None else rows_per_mod // tm
    return pl.pallas_call(
        _proj_res_kernel,
        grid=(n // tm,),
        in_specs=[pl.BlockSpec((tm, kdim), lambda i: (i, 0)),
                  pl.BlockSpec((kdim, d), lambda i: (0, 0)),
                  pl.BlockSpec((tm, d), lambda i: (i, 0)),
                  _mod_spec(2, d, tiles_per_row, const_row)],
        out_specs=pl.BlockSpec((tm, d), lambda i: (i, 0)),
        out_shape=jax.ShapeDtypeStruct((n, d), F32),
        compiler_params=_params("arbitrary"),
        name="proj_res",
    )(o, w, x, modl)


def _ffn_kernel(x_ref, g_ref, sh_ref, sc_ref, gate_ref, fg_ref, wg_ref, wu_ref, wd_ref, o_ref, h_scr,
                *, final_norm):
    j = pl.program_id(1)

    @pl.when(j == 0)
    def _():
        h_scr[...] = _norm_mod(x_ref[...], g_ref[...], sh_ref[...], sc_ref[...]).astype(BF16)
        o_ref[...] = jnp.zeros_like(o_ref)

    h = h_scr[...]
    a = jnp.dot(h, wg_ref[...], preferred_element_type=F32)
    u = jnp.dot(h, wu_ref[...], preferred_element_type=F32)
    act = (_silu(a) * u).astype(BF16)
    o_ref[...] += jnp.dot(act, wd_ref[...], preferred_element_type=F32)

    @pl.when(j == pl.num_programs(1) - 1)
    def _():
        y = x_ref[...] + gate_ref[...] * o_ref[...]
        if final_norm:
            y = _rms(y) * fg_ref[...]
        o_ref[...] = y


def _ffn(x, gain, modl, final_gain, wg, wu, wd, *, rows_per_mod, const_row, final_norm):
    n, d = x.shape
    dff = wg.shape[1]
    tm = _pick(n, 512)
    tf = _pick(dff, 512)
    tiles_per_row = None if rows_per_mod is None else rows_per_mod // tm
    vec = pl.BlockSpec((1, d), lambda i, j: (0, 0))
    ms = lambda k: _mod_spec(k, d, tiles_per_row, const_row)
    return pl.pallas_call(
        functools.partial(_ffn_kernel, final_norm=final_norm),
        grid=(n // tm, dff // tf),
        in_specs=[pl.BlockSpec((tm, d), lambda i, j: (i, 0)), vec, ms(3), ms(4), ms(5), vec,
                  pl.BlockSpec((d, tf), lambda i, j: (0, j)),
                  pl.BlockSpec((d, tf), lambda i, j: (0, j)),
                  pl.BlockSpec((tf, d), lambda i, j: (j, 0))],
        out_specs=pl.BlockSpec((tm, d), lambda i, j: (i, 0)),
        out_shape=jax.ShapeDtypeStruct((n, d), F32),
        scratch_shapes=[pltpu.VMEM((tm, d), BF16)],
        compiler_params=_params("arbitrary", "arbitrary"),
        name="ffn",
    )(x, gain, modl, modl, modl, final_gain, wg, wu, wd)


def _norm_matmul_kernel(x_ref, g_ref, sh_ref, sc_ref, w_ref, o_ref, h_scr):
    @pl.when(pl.program_id(1) == 0)
    def _():
        h_scr[...] = _norm_mod(x_ref[...], g_ref[...], sh_ref[...], sc_ref[...]).astype(BF16)

    o_ref[...] = jnp.dot(h_scr[...], w_ref[...], preferred_element_type=F32).astype(BF16)


def _norm_matmul(x, gain, modl, w, *, rows_per_mod, const_row):
    n, d = x.shape
    nout = w.shape[1]
    tm = _pick(n, 512)
    tn = _pick(nout, 1024)
    tiles_per_row = None if rows_per_mod is None else rows_per_mod // tm
    return pl.pallas_call(
        _norm_matmul_kernel,
        grid=(n // tm, nout // tn),
        in_specs=[pl.BlockSpec((tm, d), lambda i, j: (i, 0)),
                  pl.BlockSpec((1, d), lambda i, j: (0, 0)),
                  _mod_spec(0, d, tiles_per_row, const_row), _mod_spec(1, d, tiles_per_row, const_row),
                  pl.BlockSpec((d, tn), lambda i, j: (0, j))],
        out_specs=pl.BlockSpec((tm, tn), lambda i, j: (i, j)),
        out_shape=jax.ShapeDtypeStruct((n, nout), BF16),
        scratch_shapes=[pltpu.VMEM((tm, d), BF16)],
        compiler_params=_params("arbitrary", "arbitrary"),
        name="norm_matmul",
    )(x, gain, modl, modl, w)


def _na_group_starts(rows, win_r):
    slab = NA_ROW_GROUP + win_r
    return [min(max(g * NA_ROW_GROUP - win_r // 2, 0), rows - slab) for g in range(rows // NA_ROW_GROUP)]


def _na_bias_tables(rel_bias, rows, win_r, win_c):
    heads = rel_bias.shape[0]
    groups = rows // NA_ROW_GROUP
    slab = NA_ROW_GROUP + win_r
    starts = _na_group_starts(rows, win_r)
    qc = np.arange(GRID_W)
    kc = np.arange(GRID_W)
    cs = np.clip(qc - win_c // 2, 0, GRID_W - win_c)
    ok_c = (kc[:, None] >= cs[None, :]) & (kc[:, None] < cs[None, :] + win_c)
    dc = kc[:, None] - qc[None, :] + (win_c - 1)
    sel_c = (dc[None] == np.arange(2 * win_c - 1)[:, None, None]) & ok_c[None]
    by_col = jnp.einsum("hrc,ckq->hrkq", rel_bias * LOG2E, sel_c.astype(np.float32),
                        precision=lax.Precision.HIGHEST)
    by_col = jnp.where(ok_c[None, None], by_col, MASK_VALUE)
    plan = []
    for g in (0, min(1, groups - 1), groups - 1):
        table = []
        for s in range(slab):
            kr = starts[g] + s
            row = []
            for a in range(NA_ROW_GROUP):
                qr = g * NA_ROW_GROUP + a
                rs = min(max(qr - win_r // 2, 0), rows - win_r)
                row.append(kr - qr + (win_r - 1) if rs <= kr < rs + win_r else -1)
            table.append(tuple(row))
        plan.append(tuple(table))
    return pl.pallas_call(
        functools.partial(_na_bias_kernel, plan=tuple(plan)),
        grid=(heads,),
        in_specs=[pl.BlockSpec((None, 2 * win_r - 1, GRID_W, GRID_W), lambda h: (h, 0, 0, 0))],
        out_specs=pl.BlockSpec((None, 3, slab * GRID_W, NA_ROW_GROUP * GRID_W), lambda h: (h, 0, 0, 0)),
        out_shape=jax.ShapeDtypeStruct((heads, 3, slab * GRID_W, NA_ROW_GROUP * GRID_W), F32),
        compiler_params=_params("arbitrary"),
        name="na_bias",
    )(by_col)


def _na_bias_kernel(bc_ref, o_ref, *, plan):
    masked = jnp.full((GRID_W, GRID_W), MASK_VALUE, F32)
    for t, table in enumerate(plan):
        for s, row in enumerate(table):
            for a in range(0, len(row), 2):
                pair = [masked if dr < 0 else bc_ref[dr] for dr in row[a:a + 2]]
                o_ref[t, s * GRID_W:(s + 1) * GRID_W, a * GRID_W:(a + 2) * GRID_W] = jnp.concatenate(pair, axis=1)


def _na_attn_kernel(q_ref, k_ref, v_ref, kc_ref, vc_ref, bias_ref, o_ref, *, starts):
    gq = NA_ROW_GROUP * GRID_W
    nk = bias_ref.shape[1]
    kc = kc_ref[...]
    vc = vc_ref[...]
    last = len(starts) - 1

    def scores(g):
        q = q_ref[g * gq:(g + 1) * gq, :]
        k0 = starts[g] * GRID_W
        s_loc = lax.dot_general(k_ref[k0:k0 + nk, :], q, NT_DIMS, preferred_element_type=F32)
        s_ctx = lax.dot_general(kc, q, NT_DIMS, preferred_element_type=F32)
        return s_loc, s_ctx

    s_next = scores(0)
    for g, start in enumerate(starts):
        k0 = start * GRID_W
        table = 0 if g == 0 else (2 if g == last else 1)
        s_loc, s_ctx = s_next
        if g < last:
            s_next = scores(g + 1)
        s_loc = s_loc + bias_ref[table]
        m = jnp.maximum(jnp.max(s_loc, axis=0, keepdims=True), jnp.max(s_ctx, axis=0, keepdims=True))
        p_loc = jnp.exp2(s_loc - m)
        p_ctx = jnp.exp2(s_ctx - m)
        l = jnp.sum(p_loc, axis=0, keepdims=True) + jnp.sum(p_ctx, axis=0, keepdims=True)
        o_t = lax.dot_general(v_ref[k0:k0 + nk, :], p_loc.astype(BF16), TN_DIMS, preferred_element_type=F32)
        o_t = o_t + lax.dot_general(vc, p_ctx.astype(BF16), TN_DIMS, preferred_element_type=F32)
        o_ref[g * gq:(g + 1) * gq, :] = (o_t * (1.0 / l)).T.astype(BF16)


def _na_attn(qkv, kvc, bias, *, batch, heads, rows, win_r):
    s_len = qkv.shape[0] // batch
    lc = kvc.shape[0] // batch
    dh = qkv.shape[1] // (3 * heads)
    nk, gq = bias.shape[2], bias.shape[3]
    starts = _na_group_starts(rows, win_r)
    return pl.pallas_call(
        functools.partial(_na_attn_kernel, starts=starts),
        grid=(heads, batch),
        in_specs=[pl.BlockSpec((s_len, dh), lambda h, b: (b, h)),
                  pl.BlockSpec((s_len, dh), lambda h, b: (b, heads + h)),
                  pl.BlockSpec((s_len, dh), lambda h, b: (b, 2 * heads + h)),
                  pl.BlockSpec((lc, dh), lambda h, b: (b, h)),
                  pl.BlockSpec((lc, dh), lambda h, b: (b, heads + h)),
                  pl.BlockSpec((None, 3, nk, gq), lambda h, b: (h, 0, 0, 0))],
        out_specs=pl.BlockSpec((s_len, dh), lambda h, b: (b, h)),
        out_shape=jax.ShapeDtypeStruct((batch * s_len, heads * dh), BF16),
        compiler_params=_params("arbitrary", "arbitrary"),
        name="na_attn",
    )(qkv, qkv, qkv, kvc, kvc, bias)


def _rope_tables(s_len):
    t = jnp.arange(s_len)
    row = (t // GRID_W).astype(F32)
    col = (t % GRID_W).astype(F32)
    n_freq = MLA_ROPE // 4
    inv_freq = ROPE_THETA ** (-jnp.arange(n_freq, dtype=F32) / n_freq)
    ang_r = row[:, None] * inv_freq
    ang_c = col[:, None] * inv_freq
    cos = jnp.concatenate([jnp.cos(ang_r), jnp.cos(ang_r), jnp.cos(ang_c), jnp.cos(ang_c)], axis=-1)
    sin = jnp.concatenate([-jnp.sin(ang_r), jnp.sin(ang_r), -jnp.sin(ang_c), jnp.sin(ang_c)], axis=-1)
    pad = jnp.zeros((s_len, LANES - MLA_ROPE), F32)
    return jnp.concatenate([cos, pad], axis=-1), jnp.concatenate([sin, pad], axis=-1)


def _rope_swap_perm():
    q = MLA_ROPE // 4
    return jnp.concatenate([jnp.arange(q, 2 * q), jnp.arange(0, q), jnp.arange(3 * q, 4 * q), jnp.arange(2 * q, 3 * q)])


def _mla_layer(xl, xc, modl, gain, w_dq, q_norm, w_uq, w_dkv, kv_norm, w_ukv, w_o, rope, *, batch, s_len, with_ctx_out):
    d = xl.shape[1]
    qr = w_dq.shape[1]
    kvr = kv_norm.shape[0]
    heads = w_uq.shape[1] // (MLA_NOPE + MLA_ROPE)
    scale = (MLA_NOPE + MLA_ROPE) ** -0.5
    perm = _rope_swap_perm()
    pad = jnp.zeros((d, LANES - MLA_ROPE), F32)
    w_pe = w_dkv[:, kvr:]
    w_down = jnp.concatenate([w_dq, w_dkv[:, :kvr], w_pe, pad, w_pe[:, perm], pad], axis=1).astype(BF16)
    wq = w_uq.reshape(qr, heads, MLA_NOPE + MLA_ROPE)
    wq = jnp.concatenate([wq, wq[:, :, MLA_NOPE:][:, :, perm]], axis=-1) * (scale * LOG2E)
    wq = wq.reshape(qr, heads * (MLA_NOPE + LANES)).astype(BF16)
    wkv = w_ukv.reshape(kvr, heads, MLA_NOPE + MLA_V)
    wk = wkv[:, :, :MLA_NOPE].reshape(kvr, heads * MLA_NOPE).astype(BF16)
    wvt = wkv[:, :, MLA_NOPE:].reshape(kvr, heads * MLA_V).T.astype(BF16)
    qn = q_norm.reshape(1, qr)
    kvn = kv_norm.reshape(1, kvr)
    cos, sin = rope
    nc = xc.shape[0]
    cos_c = jnp.concatenate([jnp.ones((nc, MLA_ROPE), F32), jnp.zeros((nc, LANES - MLA_ROPE), F32)], axis=-1)
    sin_c = jnp.zeros((nc, LANES), F32)

    cq, ckv, kpe = _mla_down(xl, gain, modl, w_down, qn, kvn, cos, sin,
                             rows_per_mod=s_len, const_row=None, qr=qr, kvr=kvr)
    cqc, ckvc, kpec = _mla_down(xc, gain, modl, w_down, qn, kvn, cos_c, sin_c,
                                rows_per_mod=None, const_row=batch, qr=qr, kvr=kvr)
    q = _mla_qup(cq, wq, cos, sin, heads=heads)
    qc = _mla_qup(cqc, wq, cos_c, sin_c, heads=heads)
    k, vt = _mla_kvup(ckv, kpe, wk, wvt, batch=batch, heads=heads)
    kc, vtc = _mla_kvup(ckvc, kpec, wk, wvt, batch=batch, heads=heads)
    o, oc = _mla_attn(q, qc, k, kc, vt, vtc, batch=batch, heads=heads)
    w_o = w_o.astype(BF16)
    xl = _proj_res(o, w_o, xl, modl, rows_per_mod=s_len, const_row=None)
    if with_ctx_out:
        xc = _proj_res(oc, w_o, xc, modl, rows_per_mod=None, const_row=batch)
    return xl, xc


def _na_layer(xl, xc, modl, gain, w_qkv, rel_bias, w_o, *, batch, s_len, with_ctx_out):
    if with_ctx_out:
        raise NotImplementedError("neighbourhood-attention layers with a context output")
    d = xl.shape[1]
    heads = rel_bias.shape[0]
    dh = d // heads
    hd = heads * dh
    win_r = (rel_bias.shape[1] + 1) // 2
    win_c = (rel_bias.shape[2] + 1) // 2
    rows = s_len // GRID_W
    assert dh == LANES and rows % NA_ROW_GROUP == 0 and rows >= NA_ROW_GROUP + win_r
    scale = dh ** -0.5 * LOG2E
    w = jnp.concatenate([w_qkv[:, :hd] * scale, w_qkv[:, hd:]], axis=1).astype(BF16)
    qkv = _norm_matmul(xl, gain, modl, w, rows_per_mod=s_len, const_row=None)
    kvc = _norm_matmul(xc, gain, modl, w[:, hd:], rows_per_mod=None, const_row=batch)
    bias = _na_bias_tables(rel_bias, rows, win_r, win_c)
    o = _na_attn(qkv, kvc, bias, batch=batch, heads=heads, rows=rows, win_r=win_r)
    xl = _proj_res(o, w_o.astype(BF16), xl, modl, rows_per_mod=s_len, const_row=None)
    return xl, xc


def kernel(x, c, ctx, c_ctx, ada_w, ada_b, norm_mix, norm_ffn, norm_final, mla_w_dq, mla_q_norm, mla_w_uq, mla_w_dkv, mla_kv_norm, mla_w_ukv, mla_w_o, na_w_qkv, na_rel_bias, na_w_o, ffn_w_gate, ffn_w_up, ffn_w_down):
    batch, s_len, d = x.shape
    lc = ctx.shape[1]
    depth = ada_w.shape[0]
    assert batch < 8
    cc = jnp.concatenate([c, c_ctx[None, :], jnp.zeros((8 - batch - 1, d), F32)], axis=0)
    mod = _ada(cc, ada_w, ada_b).reshape(depth, 8, 6, 1, d)
    rope = _rope_tables(s_len)
    xl = x.reshape(batch * s_len, d)
    xc = ctx.reshape(batch * lc, d)
    final_gain = norm_final.reshape(1, d)
    for i in range(depth):
        last = i == depth - 1
        j = i // N_MIXERS
        modl = mod[i]
        gain = norm_mix[i].reshape(1, d)
        if i % N_MIXERS == 0:
            xl, xc = _mla_layer(xl, xc, modl, gain, mla_w_dq[j], mla_q_norm[j], mla_w_uq[j], mla_w_dkv[j],
                                mla_kv_norm[j], mla_w_ukv[j], mla_w_o[j], rope,
                                batch=batch, s_len=s_len, with_ctx_out=not last)
        else:
            xl, xc = _na_layer(xl, xc, modl, gain, na_w_qkv[j], na_rel_bias[j], na_w_o[j],
                               batch=batch, s_len=s_len, with_ctx_out=not last)
        fgain = norm_ffn[i].reshape(1, d)
        wg, wu, wd = ffn_w_gate[i].astype(BF16), ffn_w_up[i].astype(BF16), ffn_w_down[i].astype(BF16)
        xl = _ffn(xl, fgain, modl, final_gain, wg, wu, wd, rows_per_mod=s_len, const_row=None, final_norm=last)
        if not last:
            xc = _ffn(xc, fgain, modl, final_gain, wg, wu, wd, rows_per_mod=None, const_row=batch, final_norm=False)
    return xl.reshape(batch, s_len, d)
```

```python
import functools

import jax
import jax.numpy as jnp
import numpy as np
from jax import lax
from jax.experimental import pallas as pl
from jax.experimental.pallas import tpu as pltpu

GRID_W = 64
N_MIXERS = 2
MLA_NOPE = 128
MLA_ROPE = 64
MLA_V = 128
ROPE_THETA = 10000.0
EPS = 1e-6
NA_ROW_GROUP = 4
MASK_VALUE = -1e30
LOG2E = 1.4426950408889634

LANES = 128
V7X_VMEM_BYTES = 64 * 1024 * 1024
VMEM_LIMIT = V7X_VMEM_BYTES * 7 // 8

F32 = jnp.float32
BF16 = jnp.bfloat16
NT_DIMS = (((1,), (1,)), ((), ()))
TN_DIMS = (((0,), (0,)), ((), ()))


def _pick(n, pref):
    t = min(n, pref)
    while n % t:
        t //= 2
    return t


def _params(*sem):
    return pltpu.CompilerParams(dimension_semantics=sem, vmem_limit_bytes=VMEM_LIMIT)


def _rms(x):
    return x * lax.rsqrt(jnp.mean(x * x, axis=-1, keepdims=True) + EPS)


def _norm_mod(x, g, sh, sc):
    return (_rms(x) * g) * (1.0 + sc) + sh


def _silu(x):
    return x * (1.0 / (1.0 + jnp.exp(-x)))


def _mod_spec(k, d, tiles_per_row, const_row):
    if const_row is None:
        return pl.BlockSpec((None, None, 1, d), lambda i, *_: (i // tiles_per_row, k, 0, 0))
    return pl.BlockSpec((None, None, 1, d), lambda i, *_: (const_row, k, 0, 0))


def _ada_kernel(c_ref, w_ref, b_ref, o_ref):
    s = _silu(c_ref[...]).astype(BF16)
    o_ref[...] = jnp.dot(s, w_ref[...].astype(BF16), preferred_element_type=F32) + b_ref[...]


def _ada(cc, ada_w, ada_b):
    depth, d, n = ada_w.shape
    tn = _pick(n, 1024)
    return pl.pallas_call(
        _ada_kernel,
        grid=(depth, n // tn),
        in_specs=[
            pl.BlockSpec((8, d), lambda l, j: (0, 0)),
            pl.BlockSpec((None, d, tn), lambda l, j: (l, 0, j)),
            pl.BlockSpec((None, 1, tn), lambda l, j: (l, 0, j)),
        ],
        out_specs=pl.BlockSpec((None, 8, tn), lambda l, j: (l, 0, j)),
        out_shape=jax.ShapeDtypeStruct((depth, 8, n), F32),
        compiler_params=_params("arbitrary", "arbitrary"),
        name="ada_mod",
    )(cc, ada_w, ada_b.reshape(depth, 1, n))


def _mla_down_kernel(x_ref, g_ref, sh_ref, sc_ref, w_ref, qn_ref, kvn_ref, cos_ref, sin_ref,
                     cq_ref, ckv_ref, kpe_ref, *, qr, kvr):
    h = _norm_mod(x_ref[...], g_ref[...], sh_ref[...], sc_ref[...]).astype(BF16)
    a = jnp.dot(h, w_ref[...], preferred_element_type=F32)
    cq_ref[...] = (_rms(a[:, :qr]) * qn_ref[...]).astype(BF16)
    ckv_ref[...] = (_rms(a[:, qr:qr + kvr]) * kvn_ref[...]).astype(BF16)
    pe = a[:, qr + kvr:qr + kvr + LANES]
    pe_swapped = a[:, qr + kvr + LANES:]
    kpe_ref[...] = (pe * cos_ref[...] + pe_swapped * sin_ref[...]).astype(BF16)


def _mla_down(x, gain, modl, w, qn, kvn, cos, sin, *, rows_per_mod, const_row, qr, kvr):
    n, d = x.shape
    tm = _pick(n, 512)
    tiles_per_row = None if rows_per_mod is None else rows_per_mod // tm
    ntab = cos.shape[0] // tm
    vec = lambda m: pl.BlockSpec((1, m), lambda i: (0, 0))
    tab = pl.BlockSpec((tm, LANES), lambda i: (i % ntab, 0))
    row = lambda m: pl.BlockSpec((tm, m), lambda i: (i, 0))
    return pl.pallas_call(
        functools.partial(_mla_down_kernel, qr=qr, kvr=kvr),
        grid=(n // tm,),
        in_specs=[row(d), vec(d), _mod_spec(0, d, tiles_per_row, const_row),
                  _mod_spec(1, d, tiles_per_row, const_row),
                  pl.BlockSpec(w.shape, lambda i: (0, 0)), vec(qr), vec(kvr), tab, tab],
        out_specs=[row(qr), row(kvr), row(LANES)],
        out_shape=[jax.ShapeDtypeStruct((n, qr), BF16), jax.ShapeDtypeStruct((n, kvr), BF16),
                   jax.ShapeDtypeStruct((n, LANES), BF16)],
        compiler_params=_params("arbitrary"),
        name="mla_down",
    )(x, gain, modl, modl, w, qn, kvn, cos, sin)


def _mla_qup_kernel(cq_ref, w_ref, cos_ref, sin_ref, q_ref, *, heads):
    cq = cq_ref[...]
    cos = cos_ref[...]
    sin = sin_ref[...]
    qw = MLA_NOPE + LANES
    for h in range(heads):
        r = jnp.dot(cq, w_ref[:, h * qw:(h + 1) * qw], preferred_element_type=F32)
        r_pe = r[:, MLA_NOPE:]
        pe = r_pe * cos + pltpu.roll(r_pe, MLA_ROPE, 1) * sin
        q_ref[:, h * qw:h * qw + MLA_NOPE] = r[:, :MLA_NOPE].astype(BF16)
        q_ref[:, h * qw + MLA_NOPE:(h + 1) * qw] = pe.astype(BF16)


def _mla_qup(cq, wq, cos, sin, *, heads):
    n, qr = cq.shape
    nq = wq.shape[1]
    tm = _pick(n, 512)
    ntab = cos.shape[0] // tm
    return pl.pallas_call(
        functools.partial(_mla_qup_kernel, heads=heads),
        grid=(n // tm,),
        in_specs=[pl.BlockSpec((tm, qr), lambda i: (i, 0)),
                  pl.BlockSpec((qr, nq), lambda i: (0, 0)),
                  pl.BlockSpec((tm, LANES), lambda i: (i % ntab, 0)),
                  pl.BlockSpec((tm, LANES), lambda i: (i % ntab, 0))],
        out_specs=pl.BlockSpec((tm, nq), lambda i: (i, 0)),
        out_shape=jax.ShapeDtypeStruct((n, nq), BF16),
        compiler_params=_params("arbitrary"),
        name="mla_qup",
    )(cq, wq, cos, sin)


def _mla_kvup_kernel(ckv_ref, wk_ref, wvt_ref, kpe_ref, k_ref, vt_ref, *, heads):
    c = ckv_ref[...]
    kpe = kpe_ref[...]
    kw = MLA_NOPE + LANES
    k_nope = jnp.dot(c, wk_ref[...], preferred_element_type=F32).astype(BF16)
    for h in range(heads):
        k_ref[:, h * kw:h * kw + MLA_NOPE] = k_nope[:, h * MLA_NOPE:(h + 1) * MLA_NOPE]
        k_ref[:, h * kw + MLA_NOPE:(h + 1) * kw] = kpe
    vt = lax.dot_general(wvt_ref[...], c, NT_DIMS, preferred_element_type=F32)
    vt_ref[...] = vt.astype(BF16).reshape(vt_ref.shape)


def _mla_kvup(ckv, kpe, wk, wvt, *, batch, heads):
    n, kvr = ckv.shape
    per_batch = n // batch
    tm = _pick(per_batch, 512)
    tpb = per_batch // tm
    kw = MLA_NOPE + LANES
    return pl.pallas_call(
        functools.partial(_mla_kvup_kernel, heads=heads),
        grid=(n // tm,),
        in_specs=[pl.BlockSpec((tm, kvr), lambda i: (i, 0)),
                  pl.BlockSpec(wk.shape, lambda i: (0, 0)),
                  pl.BlockSpec(wvt.shape, lambda i: (0, 0)),
                  pl.BlockSpec((tm, LANES), lambda i: (i, 0))],
        out_specs=[pl.BlockSpec((tm, heads * kw), lambda i: (i, 0)),
                   pl.BlockSpec((None, heads, MLA_V, tm), lambda i: (i // tpb, 0, 0, i % tpb))],
        out_shape=[jax.ShapeDtypeStruct((n, heads * kw), BF16),
                   jax.ShapeDtypeStruct((batch, heads, MLA_V, per_batch), BF16)],
        compiler_params=_params("arbitrary"),
        name="mla_kvup",
    )(ckv, wk, wvt, kpe)


def _scores_t(chunk, q):
    k_ref, _, c0, cs = chunk
    return lax.dot_general(k_ref[pl.ds(c0, cs), :], q, NT_DIMS, preferred_element_type=F32)


def _flash_t(qs, key_chunks):
    tq = qs[0].shape[0]
    m = [jnp.full((1, tq), -jnp.inf, F32) for _ in qs]
    l = [jnp.zeros((1, tq), F32) for _ in qs]
    acc = [jnp.zeros((MLA_V, tq), F32) for _ in qs]
    s_next = [_scores_t(key_chunks[0], q) for q in qs]
    for idx, (_, vt_ref, c0, cs) in enumerate(key_chunks):
        for t, q in enumerate(qs):
            s = s_next[t]
            if idx + 1 < len(key_chunks):
                s_next[t] = _scores_t(key_chunks[idx + 1], q)
            m_new = jnp.maximum(m[t], jnp.max(s, axis=0, keepdims=True))
            alpha = jnp.exp2(m[t] - m_new)
            p = jnp.exp2(s - m_new)
            l[t] = alpha * l[t] + jnp.sum(p, axis=0, keepdims=True)
            acc[t] = alpha * acc[t] + jnp.dot(vt_ref[:, pl.ds(c0, cs)], p.astype(BF16),
                                              preferred_element_type=F32)
            m[t] = m_new
    return [a * (1.0 / d) for a, d in zip(acc, l)]


def _mla_attn_kernel(q_ref, qc_ref, k_ref, kc_ref, vt_ref, vtc_ref, o_ref, oc_ref, *, tq, tk, lockstep):
    s_len, lc = k_ref.shape[0], kc_ref.shape[0]
    chunks = [(k_ref, vt_ref, c0, tk) for c0 in range(0, s_len, tk)] + [(kc_ref, vtc_ref, 0, lc)]

    def q_tiles(i, carry):
        rows = [pl.ds(pl.multiple_of((i * lockstep + t) * tq, tq), tq) for t in range(lockstep)]
        for r, o_t in zip(rows, _flash_t([q_ref[r, :] for r in rows], chunks)):
            o_ref[r, :] = o_t.T.astype(BF16)
        return carry

    lax.fori_loop(0, s_len // (tq * lockstep), q_tiles, 0)
    oc_ref[...] = _flash_t([qc_ref[...]], [(kc_ref, vtc_ref, 0, lc)])[0].T.astype(BF16)


def _mla_attn(q, qc, k, kc, vt, vtc, *, batch, heads):
    s_len = q.shape[0] // batch
    lc = qc.shape[0] // batch
    qw = q.shape[1] // heads
    tq = _pick(s_len, 512)
    tk = _pick(s_len, 1024)
    lockstep = 2 if s_len % (2 * tq) == 0 else 1
    blk = lambda rows, w: pl.BlockSpec((rows, w), lambda b, h: (b, h))
    blk_t = lambda cols: pl.BlockSpec((None, None, MLA_V, cols), lambda b, h: (b, h, 0, 0))
    return pl.pallas_call(
        functools.partial(_mla_attn_kernel, tq=tq, tk=tk, lockstep=lockstep),
        grid=(batch, heads),
        in_specs=[blk(s_len, qw), blk(lc, qw), blk(s_len, qw), blk(lc, qw), blk_t(s_len), blk_t(lc)],
        out_specs=[blk(s_len, MLA_V), blk(lc, MLA_V)],
        out_shape=[jax.ShapeDtypeStruct((batch * s_len, heads * MLA_V), BF16),
                   jax.ShapeDtypeStruct((batch * lc, heads * MLA_V), BF16)],
        compiler_params=_params("arbitrary", "arbitrary"),
        name="mla_attn",
    )(q, qc, k, kc, vt, vtc)


def _proj_res_kernel(o_ref, w_ref, x_ref, gate_ref, out_ref):
    y = jnp.dot(o_ref[...], w_ref[...], preferred_element_type=F32)
    out_ref[...] = x_ref[...] + gate_ref[...] * y


def _proj_res(o, w, x, modl, *, rows_per_mod, const_row):
    n, d = x.shape
    kdim = o.shape[1]
    tm = _pick(n, 512)
    tiles_per_row = None if rows_per_mod is None else rows_per_mod // tm
    return pl.pallas_call(
        _proj_res_kernel,
        grid=(n // tm,),
        in_specs=[pl.BlockSpec((tm, kdim), lambda i: (i, 0)),
                  pl.BlockSpec((kdim, d), lambda i: (0, 0)),
                  pl.BlockSpec((tm, d), lambda i: (i, 0)),
                  _mod_spec(2, d, tiles_per_row, const_row)],
        out_specs=pl.BlockSpec((tm, d), lambda i: (i, 0)),
        out_shape=jax.ShapeDtypeStruct((n, d), F32),
        compiler_params=_params("arbitrary"),
        name="proj_res",
    )(o, w, x, modl)


def _ffn_kernel(x_ref, g_ref, sh_ref, sc_ref, gate_ref, fg_ref, wg_ref, wu_ref, wd_ref, o_ref, h_scr,
                *, final_norm):
    j = pl.program_id(1)

    @pl.when(j == 0)
    def _():
        h_scr[...] = _norm_mod(x_ref[...], g_ref[...], sh_ref[...], sc_ref[...]).astype(BF16)
        o_ref[...] = jnp.zeros_like(o_ref)

    h = h_scr[...]
    a = jnp.dot(h, wg_ref[...], preferred_element_type=F32)
    u = jnp.dot(h, wu_ref[...], preferred_element_type=F32)
    act = (_silu(a) * u).astype(BF16)
    o_ref[...] += jnp.dot(act, wd_ref[...], preferred_element_type=F32)

    @pl.when(j == pl.num_programs(1) - 1)
    def _():
        y = x_ref[...] + gate_ref[...] * o_ref[...]
        if final_norm:
            y = _rms(y) * fg_ref[...]
        o_ref[...] = y


def _ffn(x, gain, modl, final_gain, wg, wu, wd, *, rows_per_mod, const_row, final_norm):
    n, d = x.shape
    dff = wg.shape[1]
    tm = _pick(n, 512)
    tf = _pick(dff, 512)
    tiles_per_row = None if rows_per_mod is None else rows_per_mod // tm
    vec = pl.BlockSpec((1, d), lambda i, j: (0, 0))
    ms = lambda k: _mod_spec(k, d, tiles_per_row, const_row)
    return pl.pallas_call(
        functools.partial(_ffn_kernel, final_norm=final_norm),
        grid=(n // tm, dff // tf),
        in_specs=[pl.BlockSpec((tm, d), lambda i, j: (i, 0)), vec, ms(3), ms(4), ms(5), vec,
                  pl.BlockSpec((d, tf), lambda i, j: (0, j)),
                  pl.BlockSpec((d, tf), lambda i, j: (0, j)),
                  pl.BlockSpec((tf, d), lambda i, j: (j, 0))],
        out_specs=pl.BlockSpec((tm, d), lambda i, j: (i, 0)),
        out_shape=jax.ShapeDtypeStruct((n, d), F32),
        scratch_shapes=[pltpu.VMEM((tm, d), BF16)],
        compiler_params=_params("arbitrary", "arbitrary"),
        name="ffn",
    )(x, gain, modl, modl, modl, final_gain, wg, wu, wd)


def _norm_matmul_kernel(x_ref, g_ref, sh_ref, sc_ref, w_ref, o_ref, h_scr):
    @pl.when(pl.program_id(1) == 0)
    def _():
        h_scr[...] = _norm_mod(x_ref[...], g_ref[...], sh_ref[...], sc_ref[...]).astype(BF16)

    o_ref[...] = jnp.dot(h_scr[...], w_ref[...], preferred_element_type=F32).astype(BF16)


def _norm_matmul(x, gain, modl, w, *, rows_per_mod, const_row):
    n, d = x.shape
    nout = w.shape[1]
    tm = _pick(n, 1024)
    tn = _pick(nout, 1024)
    tiles_per_row = None if rows_per_mod is None else rows_per_mod // tm
    return pl.pallas_call(
        _norm_matmul_kernel,
        grid=(n // tm, nout // tn),
        in_specs=[pl.BlockSpec((tm, d), lambda i, j: (i, 0)),
                  pl.BlockSpec((1, d), lambda i, j: (0, 0)),
                  _mod_spec(0, d, tiles_per_row, const_row), _mod_spec(1, d, tiles_per_row, const_row),
                  pl.BlockSpec((d, tn), lambda i, j: (0, j))],
        out_specs=pl.BlockSpec((tm, tn), lambda i, j: (i, j)),
        out_shape=jax.ShapeDtypeStruct((n, nout), BF16),
        scratch_shapes=[pltpu.VMEM((tm, d), BF16)],
        compiler_params=_params("arbitrary", "arbitrary"),
        name="norm_matmul",
    )(x, gain, modl, modl, w)


def _na_group_starts(rows, win_r):
    slab = NA_ROW_GROUP + win_r
    return [min(max(g * NA_ROW_GROUP - win_r // 2, 0), rows - slab) for g in range(rows // NA_ROW_GROUP)]


def _na_bias_tables(rel_bias, rows, win_r, win_c):
    heads = rel_bias.shape[0]
    groups = rows // NA_ROW_GROUP
    slab = NA_ROW_GROUP + win_r
    starts = _na_group_starts(rows, win_r)
    qc = np.arange(GRID_W)
    kc = np.arange(GRID_W)
    cs = np.clip(qc - win_c // 2, 0, GRID_W - win_c)
    ok_c = (kc[:, None] >= cs[None, :]) & (kc[:, None] < cs[None, :] + win_c)
    dc = kc[:, None] - qc[None, :] + (win_c - 1)
    sel_c = (dc[None] == np.arange(2 * win_c - 1)[:, None, None]) & ok_c[None]
    by_col = jnp.einsum("hrc,ckq->hrkq", rel_bias * LOG2E, sel_c.astype(np.float32),
                        precision=lax.Precision.HIGHEST)
    by_col = jnp.where(ok_c[None, None], by_col, MASK_VALUE)
    plan = []
    for g in (0, min(1, groups - 1), groups - 1):
        table = []
        for s in range(slab):
            kr = starts[g] + s
            row = []
            for a in range(NA_ROW_GROUP):
                qr = g * NA_ROW_GROUP + a
                rs = min(max(qr - win_r // 2, 0), rows - win_r)
                row.append(kr - qr + (win_r - 1) if rs <= kr < rs + win_r else -1)
            table.append(tuple(row))
        plan.append(tuple(table))
    return pl.pallas_call(
        functools.partial(_na_bias_kernel, plan=tuple(plan)),
        grid=(heads,),
        in_specs=[pl.BlockSpec((None, 2 * win_r - 1, GRID_W, GRID_W), lambda h: (h, 0, 0, 0))],
        out_specs=pl.BlockSpec((None, 3, slab * GRID_W, NA_ROW_GROUP * GRID_W), lambda h: (h, 0, 0, 0)),
        out_shape=jax.ShapeDtypeStruct((heads, 3, slab * GRID_W, NA_ROW_GROUP * GRID_W), F32),
        compiler_params=_params("arbitrary"),
        name="na_bias",
    )(by_col)


def _na_bias_kernel(bc_ref, o_ref, *, plan):
    masked = jnp.full((GRID_W, GRID_W), MASK_VALUE, F32)
    for t, table in enumerate(plan):
        for s, row in enumerate(table):
            for a in range(0, len(row), 2):
                pair = [masked if dr < 0 else bc_ref[dr] for dr in row[a:a + 2]]
                o_ref[t, s * GRID_W:(s + 1) * GRID_W, a * GRID_W:(a + 2) * GRID_W] = jnp.concatenate(pair, axis=1)


def _na_attn_kernel(q_ref, k_ref, v_ref, kc_ref, vc_ref, bias_ref, o_ref, *, starts):
    gq = NA_ROW_GROUP * GRID_W
    nk = bias_ref.shape[1]
    kc = kc_ref[...]
    vc = vc_ref[...]
    last = len(starts) - 1

    def scores(g):
        q = q_ref[g * gq:(g + 1) * gq, :]
        k0 = starts[g] * GRID_W
        s_loc = lax.dot_general(k_ref[k0:k0 + nk, :], q, NT_DIMS, preferred_element_type=F32)
        s_ctx = lax.dot_general(kc, q, NT_DIMS, preferred_element_type=F32)
        return s_loc, s_ctx

    s_next = scores(0)
    for g, start in enumerate(starts):
        k0 = start * GRID_W
        table = 0 if g == 0 else (2 if g == last else 1)
        s_loc, s_ctx = s_next
        if g < last:
            s_next = scores(g + 1)
        s_loc = s_loc + bias_ref[table]
        m = jnp.maximum(jnp.max(s_loc, axis=0, keepdims=True), jnp.max(s_ctx, axis=0, keepdims=True))
        p_loc = jnp.exp2(s_loc - m)
        p_ctx = jnp.exp2(s_ctx - m)
        l = jnp.sum(p_loc, axis=0, keepdims=True) + jnp.sum(p_ctx, axis=0, keepdims=True)
        o_t = lax.dot_general(v_ref[k0:k0 + nk, :], p_loc.astype(BF16), TN_DIMS, preferred_element_type=F32)
        o_t = o_t + lax.dot_general(vc, p_ctx.astype(BF16), TN_DIMS, preferred_element_type=F32)
        o_ref[g * gq:(g + 1) * gq, :] = (o_t * (1.0 / l)).T.astype(BF16)


def _na_attn(qkv, kvc, bias, *, batch, heads, rows, win_r):
    s_len = qkv.shape[0] // batch
    lc = kvc.shape[0] // batch
    dh = qkv.shape[1] // (3 * heads)
    nk, gq = bias.shape[2], bias.shape[3]
    starts = _na_group_starts(rows, win_r)
    return pl.pallas_call(
        functools.partial(_na_attn_kernel, starts=starts),
        grid=(heads, batch),
        in_specs=[pl.BlockSpec((s_len, dh), lambda h, b: (b, h)),
                  pl.BlockSpec((s_len, dh), lambda h, b: (b, heads + h)),
                  pl.BlockSpec((s_len, dh), lambda h, b: (b, 2 * heads + h)),
                  pl.BlockSpec((lc, dh), lambda h, b: (b, h)),
                  pl.BlockSpec((lc, dh), lambda h, b: (b, heads + h)),
                  pl.BlockSpec((None, 3, nk, gq), lambda h, b: (h, 0, 0, 0))],
        out_specs=pl.BlockSpec((s_len, dh), lambda h, b: (b, h)),
        out_shape=jax.ShapeDtypeStruct((batch * s_len, heads * dh), BF16),
        compiler_params=_params("arbitrary", "arbitrary"),
        name="na_attn",
    )(qkv, qkv, qkv, kvc, kvc, bias)


def _rope_tables(s_len):
    t = jnp.arange(s_len)
    row = (t // GRID_W).astype(F32)
    col = (t % GRID_W).astype(F32)
    n_freq = MLA_ROPE // 4
    inv_freq = ROPE_THETA ** (-jnp.arange(n_freq, dtype=F32) / n_freq)
    ang_r = row[:, None] * inv_freq
    ang_c = col[:, None] * inv_freq
    cos = jnp.concatenate([jnp.cos(ang_r), jnp.cos(ang_r), jnp.cos(ang_c), jnp.cos(ang_c)], axis=-1)
    sin = jnp.concatenate([-jnp.sin(ang_r), jnp.sin(ang_r), -jnp.sin(ang_c), jnp.sin(ang_c)], axis=-1)
    pad = jnp.zeros((s_len, LANES - MLA_ROPE), F32)
    return jnp.concatenate([cos, pad], axis=-1), jnp.concatenate([sin, pad], axis=-1)


def _rope_swap_perm():
    q = MLA_ROPE // 4
    return jnp.concatenate([jnp.arange(q, 2 * q), jnp.arange(0, q), jnp.arange(3 * q, 4 * q), jnp.arange(2 * q, 3 * q)])


def _mla_layer(xl, xc, modl, gain, w_dq, q_norm, w_uq, w_dkv, kv_norm, w_ukv, w_o, rope, *, batch, s_len, with_ctx_out):
    d = xl.shape[1]
    qr = w_dq.shape[1]
    kvr = kv_norm.shape[0]
    heads = w_uq.shape[1] // (MLA_NOPE + MLA_ROPE)
    scale = (MLA_NOPE + MLA_ROPE) ** -0.5
    perm = _rope_swap_perm()
    pad = jnp.zeros((d, LANES - MLA_ROPE), F32)
    w_pe = w_dkv[:, kvr:]
    w_down = jnp.concatenate([w_dq, w_dkv[:, :kvr], w_pe, pad, w_pe[:, perm], pad], axis=1).astype(BF16)
    wq = w_uq.reshape(qr, heads, MLA_NOPE + MLA_ROPE)
    wq = jnp.concatenate([wq, wq[:, :, MLA_NOPE:][:, :, perm]], axis=-1) * (scale * LOG2E)
    wq = wq.reshape(qr, heads * (MLA_NOPE + LANES)).astype(BF16)
    wkv = w_ukv.reshape(kvr, heads, MLA_NOPE + MLA_V)
    wk = wkv[:, :, :MLA_NOPE].reshape(kvr, heads * MLA_NOPE).astype(BF16)
    wvt = wkv[:, :, MLA_NOPE:].reshape(kvr, heads * MLA_V).T.astype(BF16)
    qn = q_norm.reshape(1, qr)
    kvn = kv_norm.reshape(1, kvr)
    cos, sin = rope
    nc = xc.shape[0]
    cos_c = jnp.concatenate([jnp.ones((nc, MLA_ROPE), F32), jnp.zeros((nc, LANES - MLA_ROPE), F32)], axis=-1)
    sin_c = jnp.zeros((nc, LANES), F32)

    cq, ckv, kpe = _mla_down(xl, gain, modl, w_down, qn, kvn, cos, sin,
                             rows_per_mod=s_len, const_row=None, qr=qr, kvr=kvr)
    cqc, ckvc, kpec = _mla_down(xc, gain, modl, w_down, qn, kvn, cos_c, sin_c,
                                rows_per_mod=None, const_row=batch, qr=qr, kvr=kvr)
    q = _mla_qup(cq, wq, cos, sin, heads=heads)
    qc = _mla_qup(cqc, wq, cos_c, sin_c, heads=heads)
    k, vt = _mla_kvup(ckv, kpe, wk, wvt, batch=batch, heads=heads)
    kc, vtc = _mla_kvup(ckvc, kpec, wk, wvt, batch=batch, heads=heads)
    o, oc = _mla_attn(q, qc, k, kc, vt, vtc, batch=batch, heads=heads)
    w_o = w_o.astype(BF16)
    xl = _proj_res(o, w_o, xl, modl, rows_per_mod=s_len, const_row=None)
    if with_ctx_out:
        xc = _proj_res(oc, w_o, xc, modl, rows_per_mod=None, const_row=batch)
    return xl, xc


def _na_layer(xl, xc, modl, gain, w_qkv, rel_bias, w_o, *, batch, s_len, with_ctx_out):
    if with_ctx_out:
        raise NotImplementedError("neighbourhood-attention layers with a context output")
    d = xl.shape[1]
    heads = rel_bias.shape[0]
    dh = d // heads
    hd = heads * dh
    win_r = (rel_bias.shape[1] + 1) // 2
    win_c = (rel_bias.shape[2] + 1) // 2
    rows = s_len // GRID_W
    assert dh == LANES and rows % NA_ROW_GROUP == 0 and rows >= NA_ROW_GROUP + win_r
    scale = dh ** -0.5 * LOG2E
    w = jnp.concatenate([w_qkv[:, :hd] * scale, w_qkv[:, hd:]], axis=1).astype(BF16)
    qkv = _norm_matmul(xl, gain, modl, w, rows_per_mod=s_len, const_row=None)
    kvc = _norm_matmul(xc, gain, modl, w[:, hd:], rows_per_mod=None, const_row=batch)
    bias = _na_bias_tables(rel_bias, rows, win_r, win_c)
    o = _na_attn(qkv, kvc, bias, batch=batch, heads=heads, rows=rows, win_r=win_r)
    xl = _proj_res(o, w_o.astype(BF16), xl, modl, rows_per_mod=s_len, const_row=None)
    return xl, xc


def kernel(x, c, ctx, c_ctx, ada_w, ada_b, norm_mix, norm_ffn, norm_final, mla_w_dq, mla_q_norm, mla_w_uq, mla_w_dkv, mla_kv_norm, mla_w_ukv, mla_w_o, na_w_qkv, na_rel_bias, na_w_o, ffn_w_gate, ffn_w_up, ffn_w_down):
    batch, s_len, d = x.shape
    lc = ctx.shape[1]
    depth = ada_w.shape[0]
    assert batch < 8
    cc = jnp.concatenate([c, c_ctx[None, :], jnp.zeros((8 - batch - 1, d), F32)], axis=0)
    mod = _ada(cc, ada_w, ada_b).reshape(depth, 8, 6, 1, d)
    rope = _rope_tables(s_len)
    xl = x.reshape(batch * s_len, d)
    xc = ctx.reshape(batch * lc, d)
    final_gain = norm_final.reshape(1, d)
    for i in range(depth):
        last = i == depth - 1
        j = i // N_MIXERS
        modl = mod[i]
        gain = norm_mix[i].reshape(1, d)
        if i % N_MIXERS == 0:
            xl, xc = _mla_layer(xl, xc, modl, gain, mla_w_dq[j], mla_q_norm[j], mla_w_uq[j], mla_w_dkv[j],
                                mla_kv_norm[j], mla_w_ukv[j], mla_w_o[j], rope,
                                batch=batch, s_len=s_len, with_ctx_out=not last)
        else:
            xl, xc = _na_layer(xl, xc, modl, gain, na_w_qkv[j], na_rel_bias[j], na_w_o[j],
                               batch=batch, s_len=s_len, with_ctx_out=not last)
        fgain = norm_ffn[i].reshape(1, d)
        wg, wu, wd = ffn_w_gate[i].astype(BF16), ffn_w_up[i].astype(BF16), ffn_w_down[i].astype(BF16)
        xl = _ffn(xl, fgain, modl, final_gain, wg, wu, wd, rows_per_mod=s_len, const_row=None, final_norm=last)
        if not last:
            xc = _ffn(xc, fgain, modl, final_gain, wg, wu, wd, rows_per_mod=None, const_row=batch, final_norm=False)
    return xl.reshape(batch, s_len, d)
```
